```python
import math
import jax, jax.numpy as jnp
from jax import lax
import numpy as np

D_MODEL = 2048
BATCH = 2
SEQ = 16384
DEPTH = 1

D_MIX = D_MODEL
DA_HEADS = 8
DA_HEAD_DIM = (D_MIX // 2) // DA_HEADS
DA_QK_DIM = DA_HEAD_DIM // 2
DA_WIDTH = DA_HEADS * DA_HEAD_DIM
DA_Q_BLOCK = 128
ML_HEADS = 4
ML_V_DIM = (D_MIX - DA_WIDTH) // ML_HEADS
ML_QK_DIM = ML_V_DIM // 2
ML_WIDTH = ML_HEADS * ML_V_DIM
ML_QK_WIDTH = ML_HEADS * ML_QK_DIM
ML_CHUNK = 64
CONV_WIDTH = 4
D_FF = 4 * D_MODEL
IN_SIZES = (DA_WIDTH, DA_WIDTH, DA_WIDTH, ML_QK_WIDTH, ML_QK_WIDTH, ML_WIDTH, ML_WIDTH, ML_HEADS, ML_HEADS)
D_IN_PROJ = 3 * DA_WIDTH + 2 * ML_QK_WIDTH + 2 * ML_WIDTH + 2 * ML_HEADS
NORM_EPS = 1e-6

kernel_name = 'hybrid_mlstm_diffattn_sqrelu_layer'


def lambda_init_fn(layer_idx):
    return 0.8 - 0.6 * math.exp(-0.3 * layer_idx)


def rmsnorm(x, g):
    xf = x.astype(jnp.float32)
    y = xf * lax.rsqrt(jnp.mean(xf * xf, axis=-1, keepdims=True) + NORM_EPS)
    return (y * g.astype(jnp.float32)).astype(x.dtype)


def causal_depthwise_conv(x, w, b):
    K = w.shape[0]
    S = x.shape[1]
    xp = jnp.pad(x, ((0, 0), (K - 1, 0), (0, 0)))
    y = b
    for j in range(K):
        y = y + xp[:, j:j + S] * w[j]
    return y


def mlstm_chunkwise(q, k, v, i_pre, logf):
    B, H, S, dk = q.shape
    dv = v.shape[-1]
    L = ML_CHUNK
    NC = S // L
    q = q.reshape(B, H, NC, L, dk)
    k = k.reshape(B, H, NC, L, dk)
    v = v.reshape(B, H, NC, L, dv)
    i_pre = i_pre.reshape(B, H, NC, L)
    logf = logf.reshape(B, H, NC, L)
    b = jnp.cumsum(logf, axis=-1)
    a = b[..., -1]
    g = a[..., None] - b + i_pre
    g_max = jnp.max(g, axis=-1)
    w_state = jnp.exp(g - g_max[..., None])
    kw = k * w_state[..., None]
    C_loc = jnp.einsum('bhcsk,bhcsv->bhckv', kw, v)
    n_loc = jnp.sum(kw, axis=3)

    def step(carry, xs):
        C, n, m = carry
        C_c, n_c, a_c, gm_c = xs
        m_new = jnp.maximum(a_c + m, gm_c)
        decay = jnp.exp(a_c + m - m_new)
        scale = jnp.exp(gm_c - m_new)
        C_new = decay[..., None, None] * C + scale[..., None, None] * C_c
        n_new = decay[..., None] * n + scale[..., None] * n_c
        return (C_new, n_new, m_new), (C, n, m)

    init = (jnp.zeros((B, H, dk, dv), jnp.float32),
            jnp.zeros((B, H, dk), jnp.float32),
            jnp.zeros((B, H), jnp.float32))
    xs = (jnp.moveaxis(C_loc, 2, 0), jnp.moveaxis(n_loc, 2, 0),
          jnp.moveaxis(a, 2, 0), jnp.moveaxis(g_max, 2, 0))
    _, (C_prev, n_prev, m_prev) = lax.scan(step, init, xs)
    C_prev = jnp.moveaxis(C_prev, 0, 2)
    n_prev = jnp.moveaxis(n_prev, 0, 2)
    m_prev = jnp.moveaxis(m_prev, 0, 2)

    log_inter = b + m_prev[..., None]
    D = b[..., :, None] - b[..., None, :] + i_pre[..., None, :]
    causal = jnp.tril(jnp.ones((L, L), dtype=bool))
    D = jnp.where(causal, D, -jnp.inf)
    m_t = jnp.maximum(log_inter, jnp.max(D, axis=-1))
    inter_w = jnp.exp(log_inter - m_t)
    P = jnp.exp(D - m_t[..., None]) * jnp.einsum('bhctk,bhcsk->bhcts', q, k)
    num = (inter_w[..., None] * jnp.einsum('bhctk,bhckv->bhctv', q, C_prev)
           + jnp.einsum('bhcts,bhcsv->bhctv', P, v))
    den = inter_w * jnp.einsum('bhctk,bhck->bhct', q, n_prev) + jnp.sum(P, axis=-1)
    h = num / jnp.maximum(jnp.abs(den), jnp.exp(-m_t))[..., None]
    return h.reshape(B, H, S, dv)


def diff_attention(q, k, v, lam):
    B, H, _, S, d = q.shape
    QB = DA_Q_BLOCK
    NB = S // QB
    qb = jnp.moveaxis(q.reshape(B, H, 2, NB, QB, d), 3, 0)
    kpos = jnp.arange(S)
    scale = d ** -0.5

    def block(args):
        q_blk, start = args
        s = jnp.einsum('bhmqd,bhmsd->bhmqs', q_blk, k) * scale
        qpos = start + jnp.arange(QB)
        mask = kpos[None, :] <= qpos[:, None]
        s = jnp.where(mask, s, -jnp.inf)
        p = jax.nn.softmax(s, axis=-1)
        attn = p[:, :, 0] - lam * p[:, :, 1]
        return jnp.einsum('bhqs,bhsv->bhqv', attn, v)

    out = lax.map(block, (qb, jnp.arange(NB) * QB))
    return jnp.moveaxis(out, 0, 2).reshape(B, H, S, v.shape[-1])


def setup_inputs(seed: int = 0) -> dict:
    key = jax.random.key(seed)
    ks = jax.random.split(key, 20)
    f32 = jnp.float32
    nrm = lambda k, shape, s: jax.random.normal(k, shape, f32) * s
    b_f = (jnp.linspace(3.0, 6.0, ML_HEADS, dtype=f32)[None, :]
           + nrm(ks[6], (DEPTH, ML_HEADS), 0.1))
    return {
        'x': jax.random.normal(ks[0], (BATCH, SEQ, D_MODEL), f32),
        'norm1_g': 1.0 + nrm(ks[1], (DEPTH, D_MODEL), 0.02),
        'w_in': nrm(ks[2], (DEPTH, D_MODEL, D_IN_PROJ), D_MODEL ** -0.5),
        'ml_conv_w': nrm(ks[3], (DEPTH, CONV_WIDTH, 2 * ML_QK_WIDTH), CONV_WIDTH ** -0.5),
        'ml_conv_b': nrm(ks[4], (DEPTH, 2 * ML_QK_WIDTH), 0.02),
        'ml_b_i': nrm(ks[5], (DEPTH, ML_HEADS), 0.1),
        'ml_b_f': b_f,
        'ml_out_g': 1.0 + nrm(ks[7], (DEPTH, ML_HEADS, ML_V_DIM), 0.02),
        'da_q_norm_g': 1.0 + nrm(ks[8], (DEPTH, DA_QK_DIM), 0.02),
        'da_k_norm_g': 1.0 + nrm(ks[9], (DEPTH, DA_QK_DIM), 0.02),
        'da_lambda_q1': nrm(ks[10], (DEPTH, DA_QK_DIM), 0.1),
        'da_lambda_k1': nrm(ks[11], (DEPTH, DA_QK_DIM), 0.1),
        'da_lambda_q2': nrm(ks[12], (DEPTH, DA_QK_DIM), 0.1),
        'da_lambda_k2': nrm(ks[13], (DEPTH, DA_QK_DIM), 0.1),
        'da_out_g': 1.0 + nrm(ks[14], (DEPTH, DA_HEAD_DIM), 0.02),
        'w_out': nrm(ks[15], (DEPTH, D_MIX, D_MODEL), D_MIX ** -0.5),
        'norm2_g': 1.0 + nrm(ks[16], (DEPTH, D_MODEL), 0.02),
        'w_up': nrm(ks[17], (DEPTH, D_MODEL, D_FF), D_MODEL ** -0.5),
        'w_down': nrm(ks[18], (DEPTH, D_FF, D_MODEL), D_FF ** -0.5),
    }


def reference(x, norm1_g, w_in, ml_conv_w, ml_conv_b, ml_b_i, ml_b_f, ml_out_g,
              da_q_norm_g, da_k_norm_g, da_lambda_q1, da_lambda_k1, da_lambda_q2,
              da_lambda_k2, da_out_g, w_out, norm2_g, w_up, w_down):
    B, S, _ = x.shape
    f32 = jnp.float32
    split_points = []
    acc = 0
    for sz in IN_SIZES[:-1]:
        acc += sz
        split_points.append(acc)
    for l in range(DEPTH):
        lambda_init = lambda_init_fn(l)
        h = rmsnorm(x, norm1_g[l])
        proj = h @ w_in[l]
        da_q, da_k, da_v, ml_q, ml_k, ml_v, ml_o, ml_i, ml_f = jnp.split(proj, split_points, axis=-1)

        dq = rmsnorm(da_q.reshape(B, S, DA_HEADS, 2, DA_QK_DIM), da_q_norm_g[l])
        dk = rmsnorm(da_k.reshape(B, S, DA_HEADS, 2, DA_QK_DIM), da_k_norm_g[l])
        dq = jnp.transpose(dq, (0, 2, 3, 1, 4)).astype(f32)
        dk = jnp.transpose(dk, (0, 2, 3, 1, 4)).astype(f32)
        dv = jnp.transpose(da_v.reshape(B, S, DA_HEADS, DA_HEAD_DIM), (0, 2, 1, 3)).astype(f32)
        lam = (jnp.exp(jnp.sum(da_lambda_q1[l] * da_lambda_k1[l]).astype(f32))
               - jnp.exp(jnp.sum(da_lambda_q2[l] * da_lambda_k2[l]).astype(f32))
               + lambda_init)
        d_out = diff_attention(dq, dk, dv, lam)
        d_out = rmsnorm(jnp.transpose(d_out, (0, 2, 1, 3)), da_out_g[l]) * (1.0 - lambda_init)
        d_out = d_out.reshape(B, S, DA_WIDTH).astype(x.dtype)

        qk = causal_depthwise_conv(jnp.concatenate([ml_q, ml_k], axis=-1), ml_conv_w[l], ml_conv_b[l])
        qk = jax.nn.silu(qk)
        mq, mk = jnp.split(qk, 2, axis=-1)
        mq = jnp.transpose(mq.reshape(B, S, ML_HEADS, ML_QK_DIM), (0, 2, 1, 3)).astype(f32) * (ML_QK_DIM ** -0.5)
        mk = jnp.transpose(mk.reshape(B, S, ML_HEADS, ML_QK_DIM), (0, 2, 1, 3)).astype(f32)
        mv = jnp.transpose(ml_v.reshape(B, S, ML_HEADS, ML_V_DIM), (0, 2, 1, 3)).astype(f32)
        i_pre = jnp.transpose((ml_i + ml_b_i[l]).astype(f32), (0, 2, 1))
        logf = jnp.transpose(jax.nn.log_sigmoid((ml_f + ml_b_f[l]).astype(f32)), (0, 2, 1))
        m_h = mlstm_chunkwise(mq, mk, mv, i_pre, logf)
        m_h = rmsnorm(jnp.transpose(m_h, (0, 2, 1, 3)), ml_out_g[l]).reshape(B, S, ML_WIDTH)
        m_out = (m_h * jax.nn.sigmoid(ml_o.astype(f32))).astype(x.dtype)

        x = x + jnp.concatenate([d_out, m_out], axis=-1) @ w_out[l]

        u = rmsnorm(x, norm2_g[l]) @ w_up[l]
        x = x + jnp.square(jax.nn.relu(u)) @ w_down[l]
    return x
```

```python
import functools
import math

import jax
import jax.numpy as jnp
from jax import lax
from jax.experimental import pallas as pl
from jax.experimental.pallas import tpu as pltpu

F32 = jnp.float32
BF16 = jnp.bfloat16

D_MODEL = 2048
DA_HEADS = 8
DA_HEAD_DIM = 128
DA_QK_DIM = 64
DA_WIDTH = DA_HEADS * DA_HEAD_DIM
ML_HEADS = 4
ML_V_DIM = 256
ML_QK_DIM = 128
ML_WIDTH = ML_HEADS * ML_V_DIM
ML_QK_WIDTH = ML_HEADS * ML_QK_DIM
CONV_WIDTH = 4
D_FF = 4 * D_MODEL
D_MAIN = 3 * DA_WIDTH + 2 * ML_QK_WIDTH + 2 * ML_WIDTH
N_GATES = 2 * ML_HEADS
NORM_EPS = 1e-6
LAMBDA_INIT = 0.8 - 0.6 * math.exp(-0.3 * 0)
LANES = 128
NEG_BIG = -1e30
VMEM_LIMIT = 56 * 1024 * 1024

COL_DA_Q, COL_DA_K, COL_DA_V = 0, 1, 2
COL_ML_Q, COL_ML_K = 6, 7
COL_ML_V, COL_ML_O = 4, 5


def _cparams(sem):
    return pltpu.CompilerParams(dimension_semantics=sem, vmem_limit_bytes=VMEM_LIMIT)


def _inproj_body(x_ref, g_ref, w_ref, wg_ref, o_ref, og_ref, h_ref):
    @pl.when(pl.program_id(1) == 0)
    def _():
        x = x_ref[...]
        ms = jnp.mean(x * x, axis=-1, keepdims=True)
        hb = (x * lax.rsqrt(ms + NORM_EPS) * g_ref[...]).astype(BF16)
        h_ref[...] = hb
        og_ref[...] = jnp.dot(hb, wg_ref[...], preferred_element_type=F32)

    o_ref[...] = jnp.dot(h_ref[...], w_ref[...], preferred_element_type=F32).astype(o_ref.dtype)


def _inproj(x2, g, w_main, w_gate, bm, bn):
    T = x2.shape[0]
    return pl.pallas_call(
        _inproj_body,
        grid=(T // bm, D_MAIN // bn),
        in_specs=[
            pl.BlockSpec((bm, D_MODEL), lambda m, n: (m, 0)),
            pl.BlockSpec((1, D_MODEL), lambda m, n: (0, 0)),
            pl.BlockSpec((D_MODEL, bn), lambda m, n: (0, n)),
            pl.BlockSpec((D_MODEL, LANES), lambda m, n: (0, 0)),
        ],
        out_specs=[
            pl.BlockSpec((bm, bn), lambda m, n: (m, n)),
            pl.BlockSpec((bm, LANES), lambda m, n: (m, 0)),
        ],
        out_shape=[
            jax.ShapeDtypeStruct((T, D_MAIN), BF16),
            jax.ShapeDtypeStruct((T, LANES), F32),
        ],
        scratch_shapes=[pltpu.VMEM((bm, D_MODEL), BF16)],
        compiler_params=_cparams(("parallel", "arbitrary")),
        name="inproj",
    )(x2, g, w_main, w_gate)


def _prep_body(q_ref, k_ref, v_ref, gq_ref, gk_ref, qn_ref, kn_ref, vt_ref):
    r = lax.broadcasted_iota(jnp.int32, (LANES, LANES), 0) // DA_QK_DIM
    c = lax.broadcasted_iota(jnp.int32, (LANES, LANES), 1) // DA_QK_DIM
    group = (r == c).astype(BF16)

    def norm(x, g):
        sq = x * x
        hi = sq.astype(BF16)
        lo = (sq - hi.astype(F32)).astype(BF16)
        ss = (jnp.dot(hi, group, preferred_element_type=F32)
              + jnp.dot(lo, group, preferred_element_type=F32))
        return x * lax.rsqrt(ss * (1.0 / DA_QK_DIM) + NORM_EPS) * g

    for h in range(DA_HEADS):
        sl = slice(h * DA_HEAD_DIM, (h + 1) * DA_HEAD_DIM)
        qn_ref[:, sl] = norm(q_ref[:, sl].astype(F32), gq_ref[...]).astype(BF16)
        kn_ref[:, sl] = norm(k_ref[:, sl].astype(F32), gk_ref[...]).astype(BF16)
        vt_ref[h] = v_ref[:, sl].astype(F32).T.astype(BF16)


def _prep(proj3, gq2, gk2, tp):
    B, S, _ = proj3.shape
    nk = S // tp
    blk = lambda col: pl.BlockSpec((None, tp, DA_WIDTH), lambda b, i, col=col: (b, i, col))
    vec = pl.BlockSpec((1, LANES), lambda b, i: (0, 0))
    return pl.pallas_call(
        _prep_body,
        grid=(B, nk),
        in_specs=[blk(COL_DA_Q), blk(COL_DA_K), blk(COL_DA_V), vec, vec],
        out_specs=[
            pl.BlockSpec((None, tp, DA_WIDTH), lambda b, i: (b, i, 0)),
            pl.BlockSpec((None, tp, DA_WIDTH), lambda b, i: (b, i, 0)),
            pl.BlockSpec((None, DA_HEADS, None, DA_HEAD_DIM, tp), lambda b, i: (b, 0, i, 0, 0)),
        ],
        out_shape=[
            jax.ShapeDtypeStruct((B, S, DA_WIDTH), BF16),
            jax.ShapeDtypeStruct((B, S, DA_WIDTH), BF16),
            jax.ShapeDtypeStruct((B, DA_HEADS, nk, DA_HEAD_DIM, tp), BF16),
        ],
        compiler_params=_cparams(("parallel", "parallel")),
        name="attn_prep",
    )(proj3, proj3, proj3, gq2, gk2)


def _attn_body(lam_ref, q_ref, k_ref, vt_ref, g_ref, o_ref, m_sc, l_sc, acc_sc, *, tq, tk):
    qi = pl.program_id(2)
    q = q_ref[...]
    lane = lax.broadcasted_iota(jnp.int32, (tq, LANES), 1)
    zero = jnp.zeros_like(q)
    qq = jnp.concatenate([jnp.where(lane < DA_QK_DIM, q, zero),
                          jnp.where(lane >= DA_QK_DIM, q, zero)], axis=0)

    m_sc[...] = jnp.full(m_sc.shape, NEG_BIG, F32)
    l_sc[...] = jnp.zeros(l_sc.shape, F32)
    acc_sc[...] = jnp.zeros(acc_sc.shape, F32)

    def step(j, masked):
        kj = k_ref[pl.ds(pl.multiple_of(j * tk, tk), tk), :]
        s = lax.dot_general(kj, qq, (((1,), (1,)), ((), ())),
                            preferred_element_type=F32)
        if masked:
            kpos = j * tk + lax.broadcasted_iota(jnp.int32, (tk, 2 * tq), 0)
            col = lax.broadcasted_iota(jnp.int32, (tk, 2 * tq), 1)
            qpos = qi * tq + jnp.where(col >= tq, col - tq, col)
            s = jnp.where(kpos <= qpos, s, NEG_BIG)
        m_old = m_sc[...]
        m_new = jnp.maximum(m_old, jnp.max(s, axis=0, keepdims=True))
        p = jnp.exp(s - m_new)
        alpha = jnp.exp(m_old - m_new)
        l_sc[...] = alpha * l_sc[...] + jnp.sum(p, axis=0, keepdims=True)
        acc_sc[...] = alpha * acc_sc[...] + jnp.dot(vt_ref[j], p.astype(BF16),
                                                    preferred_element_type=F32)
        m_sc[...] = m_new

    n_diag = tq // tk
    n_full = qi * n_diag

    def full_step(j, carry):
        step(j, False)
        return carry

    lax.fori_loop(0, n_full, full_step, 0)
    for d in range(n_diag):
        step(n_full + d, True)

    lv = lam_ref[...]
    lam = (jnp.exp(jnp.sum(lv[0:1] * lv[1:2], axis=-1, keepdims=True))
           - jnp.exp(jnp.sum(lv[2:3] * lv[3:4], axis=-1, keepdims=True)) + LAMBDA_INIT)
    acc = acc_sc[...]
    l = l_sc[...]
    o = acc[:, :tq] / l[:, :tq] - lam * (acc[:, tq:] / l[:, tq:])
    ms = jnp.mean(o * o, axis=0, keepdims=True)
    y = o * lax.rsqrt(ms + NORM_EPS) * g_ref[...] * (1.0 - LAMBDA_INIT)
    o_ref[...] = y.T.astype(o_ref.dtype)


def _attn(lam4, qn, kn, vt, g_col, tq, tk):
    B, S, _ = qn.shape
    nk = S // tk
    return pl.pallas_call(
        functools.partial(_attn_body, tq=tq, tk=tk),
        grid=(B, DA_HEADS, S // tq),
        in_specs=[
            pl.BlockSpec((4, DA_QK_DIM), lambda b, h, i: (0, 0)),
            pl.BlockSpec((None, tq, DA_HEAD_DIM), lambda b, h, i: (b, i, h)),
            pl.BlockSpec((None, S, DA_HEAD_DIM), lambda b, h, i: (b, 0, h)),
            pl.BlockSpec((None, None, nk, DA_HEAD_DIM, tk), lambda b, h, i: (b, h, 0, 0, 0)),
            pl.BlockSpec((DA_HEAD_DIM, 1), lambda b, h, i: (0, 0)),
        ],
        out_specs=pl.BlockSpec((None, tq, DA_HEAD_DIM), lambda b, h, i: (b, i, h)),
        out_shape=jax.ShapeDtypeStruct((B, S, DA_WIDTH), BF16),
        scratch_shapes=[
            pltpu.VMEM((1, 2 * tq), F32),
            pltpu.VMEM((1, 2 * tq), F32),
            pltpu.VMEM((DA_HEAD_DIM, 2 * tq), F32),
        ],
        compiler_params=_cparams(("parallel", "parallel", "arbitrary")),
        name="diff_attn",
    )(lam4, qn, kn, vt, g_col)


def _log_sigmoid(x):
    return jnp.minimum(x, 0.0) - jnp.log1p(jnp.exp(-jnp.abs(x)))


def _split3(x):
    hi = x.astype(BF16)
    r1 = x - hi.astype(F32)
    mid = r1.astype(BF16)
    lo = (r1 - mid.astype(F32)).astype(BF16)
    return hi, mid, lo


def _mlstm_body(mq_ref, mk_ref, mv_ref, mo_ref, gc_ref, gr_ref, cw_ref, cb_ref, brow_ref, bcol_ref,
                og_ref, out_ref, c_sc, n_sc, m_sc, ext_sc, *, L):
    @pl.when(pl.program_id(1) == 0)
    def _():
        c_sc[...] = jnp.zeros(c_sc.shape, F32)
        n_sc[...] = jnp.zeros(n_sc.shape, F32)
        m_sc[...] = jnp.zeros(m_sc.shape, F32)
        ext_sc[0:8, :] = jnp.zeros((8, 2 * ML_QK_WIDTH), F32)

    x = jnp.concatenate([mq_ref[...].astype(F32), mk_ref[...].astype(F32)], axis=1)
    ext_sc[8:8 + L, :] = x
    cw = cw_ref[...]
    y = cb_ref[...] + cw[0:1] * ext_sc[5:5 + L, :]
    y = y + cw[1:2] * ext_sc[6:6 + L, :]
    y = y + cw[2:3] * ext_sc[7:7 + L, :]
    y = y + cw[3:4] * x
    ext_sc[0:8, :] = x[L - 8:L, :]
    qk = y * jax.nn.sigmoid(y)

    gc = gc_ref[...] + brow_ref[...]
    gr = gr_ref[...] + bcol_ref[...]
    lf_c = _log_sigmoid(gc)
    lf_r = _log_sigmoid(gr)
    ti = lax.broadcasted_iota(jnp.int32, (L, L), 0)
    si = lax.broadcasted_iota(jnp.int32, (L, L), 1)
    causal = si <= ti
    tril = causal.astype(BF16)
    triu = (ti <= si).astype(BF16)
    b_cols = sum(jnp.dot(tril, part, preferred_element_type=F32) for part in _split3(lf_c))
    b_rows = sum(jnp.dot(part, triu, preferred_element_type=F32) for part in _split3(lf_r))

    for h in range(ML_HEADS):
        q = qk[:, h * ML_QK_DIM:(h + 1) * ML_QK_DIM] * (ML_QK_DIM ** -0.5)
        k = qk[:, ML_QK_WIDTH + h * ML_QK_DIM:ML_QK_WIDTH + (h + 1) * ML_QK_DIM]
        v = mv_ref[:, h * ML_V_DIM:(h + 1) * ML_V_DIM]
        qb = q.astype(BF16)
        i_col = gc[:, h:h + 1]
        i_row = gr[h:h + 1, :]
        b_col = b_cols[:, ML_HEADS + h:ML_HEADS + h + 1]
        b_row = b_rows[ML_HEADS + h:ML_HEADS + h + 1, :]
        c_prev = c_sc[h]
        n_prev = n_sc[h]
        m_prev = m_sc[h]

        log_inter = b_col + m_prev
        dmat = jnp.where(causal, b_col - b_row + i_row, NEG_BIG)
        m_t = jnp.maximum(log_inter, jnp.max(dmat, axis=1, keepdims=True))
        inter_w = jnp.exp(log_inter - m_t)
        s_qk = lax.dot_general(qb, k.astype(BF16), (((1,), (1,)), ((), ())),
                               preferred_element_type=F32)
        p = jnp.exp(dmat - m_t) * s_qk
        num = (inter_w * jnp.dot(qb, c_prev.astype(BF16), preferred_element_type=F32)
               + jnp.dot(p.astype(BF16), v, preferred_element_type=F32))
        den = (inter_w * jnp.sum(q * n_prev, axis=1, keepdims=True)
               + jnp.sum(p, axis=1, keepdims=True))
        hh = num / jnp.maximum(jnp.abs(den), jnp.exp(-m_t))
        ms = jnp.mean(hh * hh, axis=1, keepdims=True)
        yh = hh * lax.rsqrt(ms + NORM_EPS) * og_ref[h:h + 1, :]
        gate = jax.nn.sigmoid(mo_ref[:, h * ML_V_DIM:(h + 1) * ML_V_DIM].astype(F32))
        out_ref[:, h * ML_V_DIM:(h + 1) * ML_V_DIM] = (yh * gate).astype(out_ref.dtype)

        a = b_col[L - 1:L, :]
        g_col = a - b_col + i_col
        g_max = jnp.max(g_col, axis=0, keepdims=True)
        kw = k * jnp.exp(g_col - g_max)
        c_loc = lax.dot_general(kw.astype(BF16), v, (((0,), (0,)), ((), ())),
                                preferred_element_type=F32)
        n_loc = jnp.sum(kw, axis=0, keepdims=True)
        m_new = jnp.maximum(a + m_prev, g_max)
        decay = jnp.exp(a + m_prev - m_new)
        scale = jnp.exp(g_max - m_new)
        c_sc[h] = decay * c_prev + scale * c_loc
        n_sc[h] = decay * n_prev + scale * n_loc
        m_sc[h] = m_new


def _mlstm(proj3, gates3, gates_t, conv_w, conv_b, bias_row, bias_col, out_g, L):
    B, S, _ = proj3.shape
    full = lambda shape: pl.BlockSpec(shape, lambda b, c: (0,) * len(shape))
    return pl.pallas_call(
        functools.partial(_mlstm_body, L=L),
        grid=(B, S // L),
        in_specs=[
            pl.BlockSpec((None, L, ML_QK_WIDTH), lambda b, c: (b, c, COL_ML_Q)),
            pl.BlockSpec((None, L, ML_QK_WIDTH), lambda b, c: (b, c, COL_ML_K)),
            pl.BlockSpec((None, L, ML_WIDTH), lambda b, c: (b, c, COL_ML_V)),
            pl.BlockSpec((None, L, ML_WIDTH), lambda b, c: (b, c, COL_ML_O)),
            pl.BlockSpec((None, L, LANES), lambda b, c: (b, c, 0)),
            pl.BlockSpec((None, N_GATES, L), lambda b, c: (b, 0, c)),
            full((CONV_WIDTH, 2 * ML_QK_WIDTH)),
            full((1, 2 * ML_QK_WIDTH)),
            full((1, LANES)),
            full((N_GATES, 1)),
            full((ML_HEADS, ML_V_DIM)),
        ],
        out_specs=pl.BlockSpec((None, L, ML_WIDTH), lambda b, c: (b, c, 0)),
        out_shape=jax.ShapeDtypeStruct((B, S, ML_WIDTH), BF16),
        scratch_shapes=[
            pltpu.VMEM((ML_HEADS, ML_QK_DIM, ML_V_DIM), F32),
            pltpu.VMEM((ML_HEADS, 1, ML_QK_DIM), F32),
            pltpu.VMEM((ML_HEADS, 1, 1), F32),
            pltpu.VMEM((L + 8, 2 * ML_QK_WIDTH), F32),
        ],
        compiler_params=_cparams(("parallel", "arbitrary")),
        name="mlstm",
    )(proj3, proj3, proj3, proj3, gates3, gates_t, conv_w, conv_b, bias_row, bias_col, out_g)


def _outproj_body(x_ref, d_ref, m_ref, wd_ref, wm_ref, o_ref):
    o_ref[...] = (x_ref[...]
                  + jnp.dot(d_ref[...], wd_ref[...], preferred_element_type=F32)
                  + jnp.dot(m_ref[...], wm_ref[...], preferred_element_type=F32))


def _outproj(x2, d2, m2, w_out, bm):
    T = x2.shape[0]
    return pl.pallas_call(
        _outproj_body,
        grid=(T // bm,),
        in_specs=[
            pl.BlockSpec((bm, D_MODEL), lambda m: (m, 0)),
            pl.BlockSpec((bm, DA_WIDTH), lambda m: (m, 0)),
            pl.BlockSpec((bm, ML_WIDTH), lambda m: (m, 0)),
            pl.BlockSpec((DA_WIDTH, D_MODEL), lambda m: (0, 0)),
            pl.BlockSpec((ML_WIDTH, D_MODEL), lambda m: (1, 0)),
        ],
        out_specs=pl.BlockSpec((bm, D_MODEL), lambda m: (m, 0)),
        out_shape=jax.ShapeDtypeStruct((T, D_MODEL), F32),
        compiler_params=_cparams(("parallel",)),
        name="outproj",
    )(x2, d2, m2, w_out, w_out)


def _mlp_body(x_ref, g_ref, wu_ref, wd_ref, o_ref, h_ref):
    @pl.when(pl.program_id(1) == 0)
    def _():
        x = x_ref[...]
        ms = jnp.mean(x * x, axis=-1, keepdims=True)
        h_ref[...] = (x * lax.rsqrt(ms + NORM_EPS) * g_ref[...]).astype(BF16)
        o_ref[...] = x

    u = jnp.dot(h_ref[...], wu_ref[...], preferred_element_type=F32)
    a = jnp.square(jnp.maximum(u, 0.0)).astype(BF16)
    o_ref[...] += jnp.dot(a, wd_ref[...], preferred_element_type=F32)


def _mlp(x1, g, w_up, w_down, bm, tf):
    T = x1.shape[0]
    return pl.pallas_call(
        _mlp_body,
        grid=(T // bm, D_FF // tf),
        in_specs=[
            pl.BlockSpec((bm, D_MODEL), lambda m, f: (m, 0)),
            pl.BlockSpec((1, D_MODEL), lambda m, f: (0, 0)),
            pl.BlockSpec((D_MODEL, tf), lambda m, f: (0, f)),
            pl.BlockSpec((tf, D_MODEL), lambda m, f: (f, 0)),
        ],
        out_specs=pl.BlockSpec((bm, D_MODEL), lambda m, f: (m, 0)),
        out_shape=jax.ShapeDtypeStruct((T, D_MODEL), F32),
        scratch_shapes=[pltpu.VMEM((bm, D_MODEL), BF16)],
        compiler_params=_cparams(("parallel", "arbitrary")),
        name="mlp",
    )(x1, g, w_up, w_down)


def _tiles(B, S):
    T = B * S
    return dict(
        bm_in=min(512, T), bn_in=1024,
        tp=min(512, S),
        tq=min(512, S),
        L=min(256, S),
        bm_out=min(512, T),
        bm_mlp=min(512, T), tf=1024,
    )


def kernel(x, norm1_g, w_in, ml_conv_w, ml_conv_b, ml_b_i, ml_b_f, ml_out_g, da_q_norm_g, da_k_norm_g, da_lambda_q1, da_lambda_k1, da_lambda_q2, da_lambda_k2, da_out_g, w_out, norm2_g, w_up, w_down):
    B, S, D = x.shape
    assert D == D_MODEL and norm1_g.shape[0] == 1, "single-layer kernel"
    t = _tiles(B, S)
    T = B * S
    x2 = x.reshape(T, D)

    w_in0 = w_in[0]
    w_main = w_in0[:, :D_MAIN].astype(BF16)
    w_gate = jnp.pad(w_in0[:, D_MAIN:], ((0, 0), (0, LANES - N_GATES))).astype(BF16)
    gq2 = jnp.tile(da_q_norm_g[0], 2).reshape(1, LANES) * (DA_QK_DIM ** -0.5)
    gk2 = jnp.tile(da_k_norm_g[0], 2).reshape(1, LANES)
    lam4 = jnp.stack([da_lambda_q1[0], da_lambda_k1[0], da_lambda_q2[0], da_lambda_k2[0]])
    bias8 = jnp.concatenate([ml_b_i[0], ml_b_f[0]])
    bias_row = jnp.pad(bias8, (0, LANES - N_GATES)).reshape(1, LANES)
    bias_col = bias8.reshape(N_GATES, 1)

    proj, gates = _inproj(x2, norm1_g, w_main, w_gate, t["bm_in"], t["bn_in"])
    proj3 = proj.reshape(B, S, D_MAIN)
    gates3 = gates.reshape(B, S, LANES)
    gates_t = jnp.transpose(gates3[:, :, :N_GATES], (0, 2, 1))

    qn, kn, vt = _prep(proj3, gq2, gk2, t["tp"])
    d_out = _attn(lam4, qn, kn, vt, da_out_g[0].reshape(DA_HEAD_DIM, 1), t["tq"], t["tp"])
    m_out = _mlstm(proj3, gates3, gates_t, ml_conv_w[0], ml_conv_b, bias_row, bias_col,
                   ml_out_g[0], t["L"])

    x1 = _outproj(x2, d_out.reshape(T, DA_WIDTH), m_out.reshape(T, ML_WIDTH),
                  w_out[0].astype(BF16), t["bm_out"])
    y = _mlp(x1, norm2_g, w_up[0].astype(BF16), w_down[0].astype(BF16), t["bm_mlp"], t["tf"])
    return y.reshape(B, S, D)
```

```python
import functools
import math

import jax
import jax.numpy as jnp
from jax import lax
from jax.experimental import pallas as pl
from jax.experimental.pallas import tpu as pltpu

F32 = jnp.float32
BF16 = jnp.bfloat16

D_MODEL = 2048
DA_HEADS = 8
DA_HEAD_DIM = 128
DA_QK_DIM = 64
DA_WIDTH = DA_HEADS * DA_HEAD_DIM
ML_HEADS = 4
ML_V_DIM = 256
ML_QK_DIM = 128
ML_WIDTH = ML_HEADS * ML_V_DIM
ML_QK_WIDTH = ML_HEADS * ML_QK_DIM
CONV_WIDTH = 4
D_FF = 4 * D_MODEL
D_MAIN = 3 * DA_WIDTH + 2 * ML_QK_WIDTH + 2 * ML_WIDTH
N_GATES = 2 * ML_HEADS
NORM_EPS = 1e-6
LAMBDA_INIT = 0.8 - 0.6 * math.exp(-0.3 * 0)
LANES = 128
NEG_BIG = -1e30
VMEM_LIMIT = 56 * 1024 * 1024

COL_DA_Q, COL_DA_K, COL_DA_V = 0, 1, 2
COL_ML_Q, COL_ML_K = 6, 7
COL_ML_V, COL_ML_O = 4, 5


def _cparams(sem):
    return pltpu.CompilerParams(dimension_semantics=sem, vmem_limit_bytes=VMEM_LIMIT)


def _inproj_body(x_ref, g_ref, w_ref, wg_ref, o_ref, og_ref, h_ref):
    @pl.when(pl.program_id(1) == 0)
    def _():
        x = x_ref[...]
        ms = jnp.mean(x * x, axis=-1, keepdims=True)
        hb = (x * lax.rsqrt(ms + NORM_EPS) * g_ref[...]).astype(BF16)
        h_ref[...] = hb
        og_ref[...] = jnp.dot(hb, wg_ref[...], preferred_element_type=F32)

    o_ref[...] = jnp.dot(h_ref[...], w_ref[...], preferred_element_type=F32).astype(o_ref.dtype)


def _inproj(x2, g, w_main, w_gate, bm, bn):
    T = x2.shape[0]
    return pl.pallas_call(
        _inproj_body,
        grid=(T // bm, D_MAIN // bn),
        in_specs=[
            pl.BlockSpec((bm, D_MODEL), lambda m, n: (m, 0)),
            pl.BlockSpec((1, D_MODEL), lambda m, n: (0, 0)),
            pl.BlockSpec((D_MODEL, bn), lambda m, n: (0, n)),
            pl.BlockSpec((D_MODEL, LANES), lambda m, n: (0, 0)),
        ],
        out_specs=[
            pl.BlockSpec((bm, bn), lambda m, n: (m, n)),
            pl.BlockSpec((bm, LANES), lambda m, n: (m, 0)),
        ],
        out_shape=[
            jax.ShapeDtypeStruct((T, D_MAIN), BF16),
            jax.ShapeDtypeStruct((T, LANES), F32),
        ],
        scratch_shapes=[pltpu.VMEM((bm, D_MODEL), BF16)],
        compiler_params=_cparams(("parallel", "arbitrary")),
        name="inproj",
    )(x2, g, w_main, w_gate)


def _prep_body(q_ref, k_ref, v_ref, gq_ref, gk_ref, qn_ref, kn_ref, vt_ref):
    r = lax.broadcasted_iota(jnp.int32, (LANES, LANES), 0) // DA_QK_DIM
    c = lax.broadcasted_iota(jnp.int32, (LANES, LANES), 1) // DA_QK_DIM
    group = (r == c).astype(BF16)

    def norm(x, g):
        sq = x * x
        hi = sq.astype(BF16)
        lo = (sq - hi.astype(F32)).astype(BF16)
        ss = (jnp.dot(hi, group, preferred_element_type=F32)
              + jnp.dot(lo, group, preferred_element_type=F32))
        return x * lax.rsqrt(ss * (1.0 / DA_QK_DIM) + NORM_EPS) * g

    for h in range(DA_HEADS):
        sl = slice(h * DA_HEAD_DIM, (h + 1) * DA_HEAD_DIM)
        qn_ref[:, sl] = norm(q_ref[:, sl].astype(F32), gq_ref[...]).astype(BF16)
        kn_ref[:, sl] = norm(k_ref[:, sl].astype(F32), gk_ref[...]).astype(BF16)
        vt_ref[h] = v_ref[:, sl].astype(F32).T.astype(BF16)


def _prep(proj3, gq2, gk2, tp):
    B, S, _ = proj3.shape
    nk = S // tp
    blk = lambda col: pl.BlockSpec((None, tp, DA_WIDTH), lambda b, i, col=col: (b, i, col))
    vec = pl.BlockSpec((1, LANES), lambda b, i: (0, 0))
    return pl.pallas_call(
        _prep_body,
        grid=(B, nk),
        in_specs=[blk(COL_DA_Q), blk(COL_DA_K), blk(COL_DA_V), vec, vec],
        out_specs=[
            pl.BlockSpec((None, tp, DA_WIDTH), lambda b, i: (b, i, 0)),
            pl.BlockSpec((None, tp, DA_WIDTH), lambda b, i: (b, i, 0)),
            pl.BlockSpec((None, DA_HEADS, None, DA_HEAD_DIM, tp), lambda b, i: (b, 0, i, 0, 0)),
        ],
        out_shape=[
            jax.ShapeDtypeStruct((B, S, DA_WIDTH), BF16),
            jax.ShapeDtypeStruct((B, S, DA_WIDTH), BF16),
            jax.ShapeDtypeStruct((B, DA_HEADS, nk, DA_HEAD_DIM, tp), BF16),
        ],
        compiler_params=_cparams(("parallel", "parallel")),
        name="attn_prep",
    )(proj3, proj3, proj3, gq2, gk2)


def _attn_body(lam_ref, q_ref, k_ref, vt_ref, g_ref, o_ref,
               qq_sc, s0_sc, s1_sc, p0_sc, p1_sc, a0_sc, a1_sc, c0_sc, c1_sc, m_sc, l_sc, acc_sc,
               *, tq, tk):
    qi = pl.program_id(2)
    s_sc, p_sc, a_sc, c_sc = (s0_sc, s1_sc), (p0_sc, p1_sc), (a0_sc, a1_sc), (c0_sc, c1_sc)

    qt = q_ref[...].astype(F32).T
    row = lax.broadcasted_iota(jnp.int32, (DA_HEAD_DIM, tq), 0)
    qq_sc[...] = jnp.concatenate([jnp.where(row < DA_QK_DIM, qt, 0.0),
                                  jnp.where(row >= DA_QK_DIM, qt, 0.0)], axis=1).astype(BF16)

    m_sc[...] = jnp.full(m_sc.shape, NEG_BIG, F32)
    l_sc[...] = jnp.zeros(l_sc.shape, F32)
    acc_sc[...] = jnp.zeros(acc_sc.shape, F32)
    p1_sc[...] = jnp.zeros(p1_sc.shape, BF16)
    a1_sc[...] = jnp.ones(a1_sc.shape, F32)

    def scores(t, par):
        kj = k_ref[pl.ds(pl.multiple_of(t * tk, tk), tk), :]
        s = jnp.dot(kj, qq_sc[...], preferred_element_type=F32)
        s_sc[par][...] = s
        c_sc[par][...] = jnp.max(s, axis=0, keepdims=True)

    def softmax(par, diag):
        s = s_sc[par][...]
        if diag is None:
            cmax = c_sc[par][...]
        else:
            kpos = diag * tk + lax.broadcasted_iota(jnp.int32, (tk, 2 * tq), 0)
            col = lax.broadcasted_iota(jnp.int32, (tk, 2 * tq), 1)
            s = jnp.where(kpos <= jnp.where(col >= tq, col - tq, col), s, NEG_BIG)
            cmax = jnp.max(s, axis=0, keepdims=True)
        m_old = m_sc[...]
        m_new = jnp.maximum(m_old, cmax)
        p = jnp.exp2(s - m_new)
        alpha = jnp.exp2(m_old - m_new)
        l_sc[...] = alpha * l_sc[...] + jnp.sum(p, axis=0, keepdims=True)
        m_sc[...] = m_new
        a_sc[par][...] = alpha
        p_sc[par][...] = p.astype(BF16)

    def values(t, par):
        acc_sc[...] = a_sc[par][...] * acc_sc[...] + jnp.dot(
            vt_ref[t], p_sc[par][...], preferred_element_type=F32)

    scores(0, 0)

    def pair(i, carry):
        t = 2 * i
        scores(t + 1, 1)
        softmax(0, None)
        values(jnp.maximum(t - 1, 0), 1)
        scores(t + 2, 0)
        softmax(1, None)
        values(t, 0)
        return carry

    lax.fori_loop(0, qi, pair, 0)
    t = 2 * qi
    scores(t + 1, 1)
    softmax(0, 0)
    values(jnp.maximum(t - 1, 0), 1)
    softmax(1, 1)
    values(t, 0)
    values(t + 1, 1)

    lv = lam_ref[...]
    lam = (jnp.exp(jnp.sum(lv[0:1] * lv[1:2], axis=-1, keepdims=True))
           - jnp.exp(jnp.sum(lv[2:3] * lv[3:4], axis=-1, keepdims=True)) + LAMBDA_INIT)
    acc = acc_sc[...]
    l = l_sc[...]
    o = acc[:, :tq] / l[:, :tq] - lam * (acc[:, tq:] / l[:, tq:])
    ms = jnp.mean(o * o, axis=0, keepdims=True)
    y = o * lax.rsqrt(ms + NORM_EPS) * g_ref[...] * (1.0 - LAMBDA_INIT)
    o_ref[...] = y.T.astype(o_ref.dtype)


def _attn(lam4, qn, kn, vt, g_col, tq, tk):
    B, S, _ = qn.shape
    nk = S // tk
    assert tq == 2 * tk, "a query block spans exactly two key chunks"
    return pl.pallas_call(
        functools.partial(_attn_body, tq=tq, tk=tk),
        grid=(B, DA_HEADS, S // tq),
        in_specs=[
            pl.BlockSpec((4, DA_QK_DIM), lambda b, h, i: (0, 0)),
            pl.BlockSpec((None, tq, DA_HEAD_DIM), lambda b, h, i: (b, i, h)),
            pl.BlockSpec((None, S, DA_HEAD_DIM), lambda b, h, i: (b, 0, h)),
            pl.BlockSpec((None, None, nk, DA_HEAD_DIM, tk), lambda b, h, i: (b, h, 0, 0, 0)),
            pl.BlockSpec((DA_HEAD_DIM, 1), lambda b, h, i: (0, 0)),
        ],
        out_specs=pl.BlockSpec((None, tq, DA_HEAD_DIM), lambda b, h, i: (b, i, h)),
        out_shape=jax.ShapeDtypeStruct((B, S, DA_WIDTH), BF16),
        scratch_shapes=[
            pltpu.VMEM((DA_HEAD_DIM, 2 * tq), BF16),
            pltpu.VMEM((tk, 2 * tq), F32),
            pltpu.VMEM((tk, 2 * tq), F32),
            pltpu.VMEM((tk, 2 * tq), BF16),
            pltpu.VMEM((tk, 2 * tq), BF16),
            pltpu.VMEM((1, 2 * tq), F32),
            pltpu.VMEM((1, 2 * tq), F32),
            pltpu.VMEM((1, 2 * tq), F32),
            pltpu.VMEM((1, 2 * tq), F32),
            pltpu.VMEM((1, 2 * tq), F32),
            pltpu.VMEM((1, 2 * tq), F32),
            pltpu.VMEM((DA_HEAD_DIM, 2 * tq), F32),
        ],
        compiler_params=_cparams(("parallel", "parallel", "arbitrary")),
        name="diff_attn",
    )(lam4, qn, kn, vt, g_col)


def _log_sigmoid(x):
    return jnp.minimum(x, 0.0) - jnp.log1p(jnp.exp(-jnp.abs(x)))


def _split3(x):
    hi = x.astype(BF16)
    r1 = x - hi.astype(F32)
    mid = r1.astype(BF16)
    lo = (r1 - mid.astype(F32)).astype(BF16)
    return hi, mid, lo


def _mlstm_body(mq_ref, mk_ref, mv_ref, mo_ref, gc_ref, gr_ref, cw_ref, cb_ref, brow_ref, bcol_ref,
                og_ref, out_ref, c_sc, n_sc, m_sc, ext_sc, *, L):
    @pl.when(pl.program_id(1) == 0)
    def _():
        c_sc[...] = jnp.zeros(c_sc.shape, F32)
        n_sc[...] = jnp.zeros(n_sc.shape, F32)
        m_sc[...] = jnp.zeros(m_sc.shape, F32)
        ext_sc[0:8, :] = jnp.zeros((8, 2 * ML_QK_WIDTH), F32)

    x = jnp.concatenate([mq_ref[...].astype(F32), mk_ref[...].astype(F32)], axis=1)
    ext_sc[8:8 + L, :] = x
    cw = cw_ref[...]
    y = cb_ref[...] + cw[0:1] * ext_sc[5:5 + L, :]
    y = y + cw[1:2] * ext_sc[6:6 + L, :]
    y = y + cw[2:3] * ext_sc[7:7 + L, :]
    y = y + cw[3:4] * x
    ext_sc[0:8, :] = x[L - 8:L, :]
    qk = y * jax.nn.sigmoid(y)

    gc = gc_ref[...] + brow_ref[...]
    gr = gr_ref[...] + bcol_ref[...]
    lf_c = _log_sigmoid(gc)
    lf_r = _log_sigmoid(gr)
    ti = lax.broadcasted_iota(jnp.int32, (L, L), 0)
    si = lax.broadcasted_iota(jnp.int32, (L, L), 1)
    causal = si <= ti
    tril = causal.astype(BF16)
    triu = (ti <= si).astype(BF16)
    b_cols = sum(jnp.dot(tril, part, preferred_element_type=F32) for part in _split3(lf_c))
    b_rows = sum(jnp.dot(part, triu, preferred_element_type=F32) for part in _split3(lf_r))

    for h in range(ML_HEADS):
        q = qk[:, h * ML_QK_DIM:(h + 1) * ML_QK_DIM] * (ML_QK_DIM ** -0.5)
        k = qk[:, ML_QK_WIDTH + h * ML_QK_DIM:ML_QK_WIDTH + (h + 1) * ML_QK_DIM]
        v = mv_ref[:, h * ML_V_DIM:(h + 1) * ML_V_DIM]
        qb = q.astype(BF16)
        i_col = gc[:, h:h + 1]
        i_row = gr[h:h + 1, :]
        b_col = b_cols[:, ML_HEADS + h:ML_HEADS + h + 1]
        b_row = b_rows[ML_HEADS + h:ML_HEADS + h + 1, :]
        c_prev = c_sc[h]
        n_prev = n_sc[h]
        m_prev = m_sc[h]

        log_inter = b_col + m_prev
        dmat = jnp.where(causal, b_col - b_row + i_row, NEG_BIG)
        m_t = jnp.maximum(log_inter, jnp.max(dmat, axis=1, keepdims=True))
        inter_w = jnp.exp(log_inter - m_t)
        s_qk = lax.dot_general(qb, k.astype(BF16), (((1,), (1,)), ((), ())),
                               preferred_element_type=F32)
        p = jnp.exp(dmat - m_t) * s_qk
        num = (inter_w * jnp.dot(qb, c_prev.astype(BF16), preferred_element_type=F32)
               + jnp.dot(p.astype(BF16), v, preferred_element_type=F32))
        den = (inter_w * jnp.sum(q * n_prev, axis=1, keepdims=True)
               + jnp.sum(p, axis=1, keepdims=True))
        hh = num / jnp.maximum(jnp.abs(den), jnp.exp(-m_t))
        ms = jnp.mean(hh * hh, axis=1, keepdims=True)
        yh = hh * lax.rsqrt(ms + NORM_EPS) * og_ref[h:h + 1, :]
        gate = jax.nn.sigmoid(mo_ref[:, h * ML_V_DIM:(h + 1) * ML_V_DIM].astype(F32))
        out_ref[:, h * ML_V_DIM:(h + 1) * ML_V_DIM] = (yh * gate).astype(out_ref.dtype)

        a = b_col[L - 1:L, :]
        g_col = a - b_col + i_col
        g_max = jnp.max(g_col, axis=0, keepdims=True)
        kw = k * jnp.exp(g_col - g_max)
        c_loc = lax.dot_general(kw.astype(BF16), v, (((0,), (0,)), ((), ())),
                                preferred_element_type=F32)
        n_loc = jnp.sum(kw, axis=0, keepdims=True)
        m_new = jnp.maximum(a + m_prev, g_max)
        decay = jnp.exp(a + m_prev - m_new)
        scale = jnp.exp(g_max - m_new)
        c_sc[h] = decay * c_prev + scale * c_loc
        n_sc[h] = decay * n_prev + scale * n_loc
        m_sc[h] = m_new


def _mlstm(proj3, gates3, gates_t, conv_w, conv_b, bias_row, bias_col, out_g, L):
    B, S, _ = proj3.shape
    full = lambda shape: pl.BlockSpec(shape, lambda b, c: (0,) * len(shape))
    return pl.pallas_call(
        functools.partial(_mlstm_body, L=L),
        grid=(B, S // L),
        in_specs=[
            pl.BlockSpec((None, L, ML_QK_WIDTH), lambda b, c: (b, c, COL_ML_Q)),
            pl.BlockSpec((None, L, ML_QK_WIDTH), lambda b, c: (b, c, COL_ML_K)),
            pl.BlockSpec((None, L, ML_WIDTH), lambda b, c: (b, c, COL_ML_V)),
            pl.BlockSpec((None, L, ML_WIDTH), lambda b, c: (b, c, COL_ML_O)),
            pl.BlockSpec((None, L, LANES), lambda b, c: (b, c, 0)),
            pl.BlockSpec((None, N_GATES, L), lambda b, c: (b, 0, c)),
            full((CONV_WIDTH, 2 * ML_QK_WIDTH)),
            full((1, 2 * ML_QK_WIDTH)),
            full((1, LANES)),
            full((N_GATES, 1)),
            full((ML_HEADS, ML_V_DIM)),
        ],
        out_specs=pl.BlockSpec((None, L, ML_WIDTH), lambda b, c: (b, c, 0)),
        out_shape=jax.ShapeDtypeStruct((B, S, ML_WIDTH), BF16),
        scratch_shapes=[
            pltpu.VMEM((ML_HEADS, ML_QK_DIM, ML_V_DIM), F32),
            pltpu.VMEM((ML_HEADS, 1, ML_QK_DIM), F32),
            pltpu.VMEM((ML_HEADS, 1, 1), F32),
            pltpu.VMEM((L + 8, 2 * ML_QK_WIDTH), F32),
        ],
        compiler_params=_cparams(("parallel", "arbitrary")),
        name="mlstm",
    )(proj3, proj3, proj3, proj3, gates3, gates_t, conv_w, conv_b, bias_row, bias_col, out_g)


def _outproj_body(x_ref, d_ref, m_ref, wd_ref, wm_ref, o_ref):
    o_ref[...] = (x_ref[...]
                  + jnp.dot(d_ref[...], wd_ref[...], preferred_element_type=F32)
                  + jnp.dot(m_ref[...], wm_ref[...], preferred_element_type=F32))


def _outproj(x2, d2, m2, w_out, bm):
    T = x2.shape[0]
    return pl.pallas_call(
        _outproj_body,
        grid=(T // bm,),
        in_specs=[
            pl.BlockSpec((bm, D_MODEL), lambda m: (m, 0)),
            pl.BlockSpec((bm, DA_WIDTH), lambda m: (m, 0)),
            pl.BlockSpec((bm, ML_WIDTH), lambda m: (m, 0)),
            pl.BlockSpec((DA_WIDTH, D_MODEL), lambda m: (0, 0)),
            pl.BlockSpec((ML_WIDTH, D_MODEL), lambda m: (1, 0)),
        ],
        out_specs=pl.BlockSpec((bm, D_MODEL), lambda m: (m, 0)),
        out_shape=jax.ShapeDtypeStruct((T, D_MODEL), F32),
        compiler_params=_cparams(("parallel",)),
        name="outproj",
    )(x2, d2, m2, w_out, w_out)


def _mlp_body(x_ref, g_ref, wu_ref, wd_ref, o_ref, h_ref):
    @pl.when(pl.program_id(1) == 0)
    def _():
        x = x_ref[...]
        ms = jnp.mean(x * x, axis=-1, keepdims=True)
        h_ref[...] = (x * lax.rsqrt(ms + NORM_EPS) * g_ref[...]).astype(BF16)
        o_ref[...] = x

    u = jnp.dot(h_ref[...], wu_ref[...], preferred_element_type=F32)
    a = jnp.square(jnp.maximum(u, 0.0)).astype(BF16)
    o_ref[...] += jnp.dot(a, wd_ref[...], preferred_element_type=F32)


def _mlp(x1, g, w_up, w_down, bm, tf):
    T = x1.shape[0]
    return pl.pallas_call(
        _mlp_body,
        grid=(T // bm, D_FF // tf),
        in_specs=[
            pl.BlockSpec((bm, D_MODEL), lambda m, f: (m, 0)),
            pl.BlockSpec((1, D_MODEL), lambda m, f: (0, 0)),
            pl.BlockSpec((D_MODEL, tf), lambda m, f: (0, f)),
            pl.BlockSpec((tf, D_MODEL), lambda m, f: (f, 0)),
        ],
        out_specs=pl.BlockSpec((bm, D_MODEL), lambda m, f: (m, 0)),
        out_shape=jax.ShapeDtypeStruct((T, D_MODEL), F32),
        scratch_shapes=[pltpu.VMEM((bm, D_MODEL), BF16)],
        compiler_params=_cparams(("parallel", "arbitrary")),
        name="mlp",
    )(x1, g, w_up, w_down)


def _tiles(B, S):
    T = B * S
    return dict(
        bm_in=min(512, T), bn_in=1024,
        tp=min(512, S // 2),
        tq=min(1024, S),
        L=min(256, S),
        bm_out=min(512, T),
        bm_mlp=min(512, T), tf=1024,
    )


def kernel(x, norm1_g, w_in, ml_conv_w, ml_conv_b, ml_b_i, ml_b_f, ml_out_g, da_q_norm_g, da_k_norm_g, da_lambda_q1, da_lambda_k1, da_lambda_q2, da_lambda_k2, da_out_g, w_out, norm2_g, w_up, w_down):
    B, S, D = x.shape
    assert D == D_MODEL and norm1_g.shape[0] == 1, "single-layer kernel"
    t = _tiles(B, S)
    T = B * S
    x2 = x.reshape(T, D)

    w_in0 = w_in[0]
    w_main = w_in0[:, :D_MAIN].astype(BF16)
    w_gate = jnp.pad(w_in0[:, D_MAIN:], ((0, 0), (0, LANES - N_GATES))).astype(BF16)
    gq2 = jnp.tile(da_q_norm_g[0], 2).reshape(1, LANES) * (DA_QK_DIM ** -0.5 * math.log2(math.e))
    gk2 = jnp.tile(da_k_norm_g[0], 2).reshape(1, LANES)
    lam4 = jnp.stack([da_lambda_q1[0], da_lambda_k1[0], da_lambda_q2[0], da_lambda_k2[0]])
    bias8 = jnp.concatenate([ml_b_i[0], ml_b_f[0]])
    bias_row = jnp.pad(bias8, (0, LANES - N_GATES)).reshape(1, LANES)
    bias_col = bias8.reshape(N_GATES, 1)

    proj, gates = _inproj(x2, norm1_g, w_main, w_gate, t["bm_in"], t["bn_in"])
    proj3 = proj.reshape(B, S, D_MAIN)
    gates3 = gates.reshape(B, S, LANES)
    gates_t = jnp.transpose(gates3[:, :, :N_GATES], (0, 2, 1))

    qn, kn, vt = _prep(proj3, gq2, gk2, t["tp"])
    d_out = _attn(lam4, qn, kn, vt, da_out_g[0].reshape(DA_HEAD_DIM, 1), t["tq"], t["tp"])
    m_out = _mlstm(proj3, gates3, gates_t, ml_conv_w[0], ml_conv_b, bias_row, bias_col,
                   ml_out_g[0], t["L"])

    x1 = _outproj(x2, d_out.reshape(T, DA_WIDTH), m_out.reshape(T, ML_WIDTH),
                  w_out[0].astype(BF16), t["bm_out"])
    y = _mlp(x1, norm2_g, w_up[0].astype(BF16), w_down[0].astype(BF16), t["bm_mlp"], t["tf"])
    return y.reshape(B, S, D)
```

```python
import functools
import math

import jax
import jax.numpy as jnp
from jax import lax
from jax.experimental import pallas as pl
from jax.experimental.pallas import tpu as pltpu

F32 = jnp.float32
BF16 = jnp.bfloat16

D_MODEL = 2048
DA_HEADS = 8
DA_HEAD_DIM = 128
DA_QK_DIM = 64
DA_WIDTH = DA_HEADS * DA_HEAD_DIM
ML_HEADS = 4
ML_V_DIM = 256
ML_QK_DIM = 128
ML_WIDTH = ML_HEADS * ML_V_DIM
ML_QK_WIDTH = ML_HEADS * ML_QK_DIM
CONV_WIDTH = 4
D_FF = 4 * D_MODEL
D_MAIN = 3 * DA_WIDTH + 2 * ML_QK_WIDTH + 2 * ML_WIDTH
N_GATES = 2 * ML_HEADS
NORM_EPS = 1e-6
LAMBDA_INIT = 0.8 - 0.6 * math.exp(-0.3 * 0)
LANES = 128
STRIP = 256
NEG_BIG = -1e30
SCORE_BOUND = 60.0
VMEM_LIMIT = 56 * 1024 * 1024

COL_DA_Q, COL_DA_K, COL_DA_V = 0, 1, 2
COL_ML_Q, COL_ML_K = 6, 7
COL_ML_V, COL_ML_O = 4, 5


def _cparams(sem):
    return pltpu.CompilerParams(dimension_semantics=sem, vmem_limit_bytes=VMEM_LIMIT)


def _inproj_body(x_ref, g_ref, w_ref, wg_ref, o_ref, og_ref, h_ref):
    @pl.when(pl.program_id(1) == 0)
    def _():
        x = x_ref[...]
        ms = jnp.mean(x * x, axis=-1, keepdims=True)
        hb = (x * lax.rsqrt(ms + NORM_EPS) * g_ref[...]).astype(BF16)
        h_ref[...] = hb
        og_ref[...] = jnp.dot(hb, wg_ref[...], preferred_element_type=F32)

    o_ref[...] = jnp.dot(h_ref[...], w_ref[...], preferred_element_type=F32).astype(o_ref.dtype)


def _inproj(x2, g, w_main, w_gate, bm, bn):
    T = x2.shape[0]
    return pl.pallas_call(
        _inproj_body,
        grid=(T // bm, D_MAIN // bn),
        in_specs=[
            pl.BlockSpec((bm, D_MODEL), lambda m, n: (m, 0)),
            pl.BlockSpec((1, D_MODEL), lambda m, n: (0, 0)),
            pl.BlockSpec((D_MODEL, bn), lambda m, n: (0, n)),
            pl.BlockSpec((D_MODEL, LANES), lambda m, n: (0, 0)),
        ],
        out_specs=[
            pl.BlockSpec((bm, bn), lambda m, n: (m, n)),
            pl.BlockSpec((bm, LANES), lambda m, n: (m, 0)),
        ],
        out_shape=[
            jax.ShapeDtypeStruct((T, D_MAIN), BF16),
            jax.ShapeDtypeStruct((T, LANES), F32),
        ],
        scratch_shapes=[pltpu.VMEM((bm, D_MODEL), BF16)],
        compiler_params=_cparams(("parallel", "arbitrary")),
        name="inproj",
    )(x2, g, w_main, w_gate)


def _prep_body(q_ref, k_ref, v_ref, gq_ref, gk_ref, qn_ref, kn_ref, vt_ref):
    r = lax.broadcasted_iota(jnp.int32, (LANES, LANES), 0) // DA_QK_DIM
    c = lax.broadcasted_iota(jnp.int32, (LANES, LANES), 1) // DA_QK_DIM
    group = (r == c).astype(BF16)

    def norm(x, g):
        sq = x * x
        hi = sq.astype(BF16)
        lo = (sq - hi.astype(F32)).astype(BF16)
        ss = (jnp.dot(hi, group, preferred_element_type=F32)
              + jnp.dot(lo, group, preferred_element_type=F32))
        return x * lax.rsqrt(ss * (1.0 / DA_QK_DIM) + NORM_EPS) * g

    for h in range(DA_HEADS):
        sl = slice(h * DA_HEAD_DIM, (h + 1) * DA_HEAD_DIM)
        qn_ref[:, sl] = norm(q_ref[:, sl].astype(F32), gq_ref[...]).astype(BF16)
        kn_ref[:, sl] = norm(k_ref[:, sl].astype(F32), gk_ref[...]).astype(BF16)
        vt_ref[h] = v_ref[:, sl].astype(F32).T.astype(BF16)


def _prep(proj3, gq2, gk2, tp):
    B, S, _ = proj3.shape
    nk = S // tp
    blk = lambda col: pl.BlockSpec((None, tp, DA_WIDTH), lambda b, i, col=col: (b, i, col))
    vec = pl.BlockSpec((1, LANES), lambda b, i: (0, 0))
    return pl.pallas_call(
        _prep_body,
        grid=(B, nk),
        in_specs=[blk(COL_DA_Q), blk(COL_DA_K), blk(COL_DA_V), vec, vec],
        out_specs=[
            pl.BlockSpec((None, tp, DA_WIDTH), lambda b, i: (b, i, 0)),
            pl.BlockSpec((None, tp, DA_WIDTH), lambda b, i: (b, i, 0)),
            pl.BlockSpec((None, DA_HEADS, None, DA_HEAD_DIM, tp), lambda b, i: (b, 0, i, 0, 0)),
        ],
        out_shape=[
            jax.ShapeDtypeStruct((B, S, DA_WIDTH), BF16),
            jax.ShapeDtypeStruct((B, S, DA_WIDTH), BF16),
            jax.ShapeDtypeStruct((B, DA_HEADS, nk, DA_HEAD_DIM, tp), BF16),
        ],
        compiler_params=_cparams(("parallel", "parallel")),
        name="attn_prep",
    )(proj3, proj3, proj3, gq2, gk2)


def _attn_body(bounded_ref, lam_ref, q_ref, k_ref, vt_ref, g_ref, o_ref,
               qq_sc, s0_sc, s1_sc, p0_sc, p1_sc, a0_sc, a1_sc, c0_sc, c1_sc, m_sc, l_sc, acc_sc,
               *, tq, tk):
    qi = pl.program_id(2)
    s_sc, p_sc, a_sc, c_sc = (s0_sc, s1_sc), (p0_sc, p1_sc), (a0_sc, a1_sc), (c0_sc, c1_sc)

    qt = q_ref[...].astype(F32).T
    row = lax.broadcasted_iota(jnp.int32, (DA_HEAD_DIM, tq), 0)
    qq_sc[...] = jnp.concatenate([jnp.where(row < DA_QK_DIM, qt, 0.0),
                                  jnp.where(row >= DA_QK_DIM, qt, 0.0)], axis=1).astype(BF16)

    l_sc[...] = jnp.zeros(l_sc.shape, F32)
    acc_sc[...] = jnp.zeros(acc_sc.shape, F32)

    strips = [slice(c * STRIP, (c + 1) * STRIP) for c in range(2 * tq // STRIP)]

    def key_chunk(t):
        return k_ref[pl.ds(pl.multiple_of(t * tk, tk), tk), :]

    def causal(diag, sl):
        q_lo = sl.start % tq
        if q_lo >= (diag + 1) * tk - 1:
            return None
        kpos = diag * tk + lax.broadcasted_iota(jnp.int32, (tk, STRIP), 0)
        qpos = q_lo + lax.broadcasted_iota(jnp.int32, (tk, STRIP), 1)
        return kpos <= qpos

    def visible(diag, sl):
        return (sl.start % tq) + STRIP - 1 >= diag * tk

    def probs(t, par, diag=None):
        kj = key_chunk(t)
        for sl in strips:
            if diag is not None and not visible(diag, sl):
                continue
            p = jnp.exp2(jnp.dot(kj, qq_sc[:, sl], preferred_element_type=F32))
            mask = None if diag is None else causal(diag, sl)
            if mask is not None:
                p = jnp.where(mask, p, 0.0)
            l_sc[:, sl] += jnp.sum(p, axis=0, keepdims=True)
            p_sc[par][:, sl] = p.astype(BF16)

    def values_plain(t, par, diag=None):
        vt = vt_ref[t]
        for sl in strips:
            if diag is not None and not visible(diag, sl):
                continue
            acc_sc[:, sl] += jnp.dot(vt, p_sc[par][:, sl], preferred_element_type=F32)

    def bounded_path():
        d0 = 2 * qi
        probs(d0, 0, diag=0)
        probs(d0 + 1, 1, diag=1)
        values_plain(d0, 0, diag=0)

        @pl.when(qi == 0)
        def _():
            values_plain(d0 + 1, 1, diag=1)

        @pl.when(qi > 0)
        def _():
            probs(0, 0)
            values_plain(d0 + 1, 1, diag=1)

            def pair(i, carry):
                t = 2 * i
                probs(t + 1, 1)
                values_plain(t, 0)
                probs(t + 2, 0)
                values_plain(t + 1, 1)
                return carry

            lax.fori_loop(0, qi - 1, pair, 0)
            t = 2 * qi - 2
            probs(t + 1, 1)
            values_plain(t, 0)
            values_plain(t + 1, 1)

    def scores(t, par):
        kj = key_chunk(t)
        for sl in strips:
            s = jnp.dot(kj, qq_sc[:, sl], preferred_element_type=F32)
            s_sc[par][:, sl] = s
            c_sc[par][:, sl] = jnp.max(s, axis=0, keepdims=True)

    def softmax(par, diag):
        for sl in strips:
            s = s_sc[par][:, sl]
            mask = None if diag is None else causal(diag, sl)
            if mask is None:
                cmax = c_sc[par][:, sl]
            else:
                s = jnp.where(mask, s, NEG_BIG)
                cmax = jnp.max(s, axis=0, keepdims=True)
            m_old = m_sc[:, sl]
            m_new = jnp.maximum(m_old, cmax)
            p = jnp.exp2(s - m_new)
            alpha = jnp.exp2(m_old - m_new)
            l_sc[:, sl] = alpha * l_sc[:, sl] + jnp.sum(p, axis=0, keepdims=True)
            m_sc[:, sl] = m_new
            a_sc[par][:, sl] = alpha
            p_sc[par][:, sl] = p.astype(BF16)

    def values(t, par):
        vt = vt_ref[t]
        for sl in strips:
            acc_sc[:, sl] = a_sc[par][:, sl] * acc_sc[:, sl] + jnp.dot(
                vt, p_sc[par][:, sl], preferred_element_type=F32)

    def online_path():
        m_sc[...] = jnp.full(m_sc.shape, NEG_BIG, F32)
        p1_sc[...] = jnp.zeros(p1_sc.shape, BF16)
        a1_sc[...] = jnp.ones(a1_sc.shape, F32)
        scores(0, 0)

        def pair(i, carry):
            t = 2 * i
            scores(t + 1, 1)
            softmax(0, None)
            values(jnp.maximum(t - 1, 0), 1)
            scores(t + 2, 0)
            softmax(1, None)
            values(t, 0)
            return carry

        lax.fori_loop(0, qi, pair, 0)
        t = 2 * qi
        scores(t + 1, 1)
        softmax(0, 0)
        values(jnp.maximum(t - 1, 0), 1)
        softmax(1, 1)
        values(t, 0)
        values(t + 1, 1)

    pl.when(bounded_ref[0] == 1)(bounded_path)
    pl.when(bounded_ref[0] != 1)(online_path)

    lv = lam_ref[...]
    lam = (jnp.exp(jnp.sum(lv[0:1] * lv[1:2], axis=-1, keepdims=True))
           - jnp.exp(jnp.sum(lv[2:3] * lv[3:4], axis=-1, keepdims=True)) + LAMBDA_INIT)
    acc = acc_sc[...]
    l = l_sc[...]
    o = acc[:, :tq] / l[:, :tq] - lam * (acc[:, tq:] / l[:, tq:])
    ms = jnp.mean(o * o, axis=0, keepdims=True)
    y = o * lax.rsqrt(ms + NORM_EPS) * g_ref[...] * (1.0 - LAMBDA_INIT)
    o_ref[...] = y.T.astype(o_ref.dtype)


def _attn(bounded, lam4, qn, kn, vt, g_col, tq, tk):
    B, S, _ = qn.shape
    nk = S // tk
    assert tq == 2 * tk, "a query block spans exactly two key chunks"
    grid_spec = pltpu.PrefetchScalarGridSpec(
        num_scalar_prefetch=1,
        grid=(B, DA_HEADS, S // tq),
        in_specs=[
            pl.BlockSpec((4, DA_QK_DIM), lambda b, h, i, f: (0, 0)),
            pl.BlockSpec((None, tq, DA_HEAD_DIM), lambda b, h, i, f: (b, i, h)),
            pl.BlockSpec((None, S, DA_HEAD_DIM), lambda b, h, i, f: (b, 0, h)),
            pl.BlockSpec((None, None, nk, DA_HEAD_DIM, tk), lambda b, h, i, f: (b, h, 0, 0, 0)),
            pl.BlockSpec((DA_HEAD_DIM, 1), lambda b, h, i, f: (0, 0)),
        ],
        out_specs=pl.BlockSpec((None, tq, DA_HEAD_DIM), lambda b, h, i, f: (b, i, h)),
        scratch_shapes=[
            pltpu.VMEM((DA_HEAD_DIM, 2 * tq), BF16),
            pltpu.VMEM((tk, 2 * tq), F32),
            pltpu.VMEM((tk, 2 * tq), F32),
            pltpu.VMEM((tk, 2 * tq), BF16),
            pltpu.VMEM((tk, 2 * tq), BF16),
            pltpu.VMEM((1, 2 * tq), F32),
            pltpu.VMEM((1, 2 * tq), F32),
            pltpu.VMEM((1, 2 * tq), F32),
            pltpu.VMEM((1, 2 * tq), F32),
            pltpu.VMEM((1, 2 * tq), F32),
            pltpu.VMEM((1, 2 * tq), F32),
            pltpu.VMEM((DA_HEAD_DIM, 2 * tq), F32),
        ],
    )
    return pl.pallas_call(
        functools.partial(_attn_body, tq=tq, tk=tk),
        grid_spec=grid_spec,
        out_shape=jax.ShapeDtypeStruct((B, S, DA_WIDTH), BF16),
        compiler_params=_cparams(("parallel", "parallel", "arbitrary")),
        name="diff_attn",
    )(bounded, lam4, qn, kn, vt, g_col)


def _log_sigmoid(x):
    return jnp.minimum(x, 0.0) - jnp.log1p(jnp.exp(-jnp.abs(x)))


def _split3(x):
    hi = x.astype(BF16)
    r1 = x - hi.astype(F32)
    mid = r1.astype(BF16)
    lo = (r1 - mid.astype(F32)).astype(BF16)
    return hi, mid, lo


def _mlstm_body(mq_ref, mk_ref, mv_ref, mo_ref, gc_ref, gr_ref, cw_ref, cb_ref, brow_ref, bcol_ref,
                og_ref, out_ref, c_sc, n_sc, m_sc, ext_sc, *, L):
    @pl.when(pl.program_id(1) == 0)
    def _():
        c_sc[...] = jnp.zeros(c_sc.shape, F32)
        n_sc[...] = jnp.zeros(n_sc.shape, F32)
        m_sc[...] = jnp.zeros(m_sc.shape, F32)
        ext_sc[0:8, :] = jnp.zeros((8, 2 * ML_QK_WIDTH), F32)

    x = jnp.concatenate([mq_ref[...].astype(F32), mk_ref[...].astype(F32)], axis=1)
    ext_sc[8:8 + L, :] = x
    cw = cw_ref[...]
    y = cb_ref[...] + cw[0:1] * ext_sc[5:5 + L, :]
    y = y + cw[1:2] * ext_sc[6:6 + L, :]
    y = y + cw[2:3] * ext_sc[7:7 + L, :]
    y = y + cw[3:4] * x
    ext_sc[0:8, :] = x[L - 8:L, :]
    qk = y * jax.nn.sigmoid(y)

    gc = gc_ref[...] + brow_ref[...]
    gr = gr_ref[...] + bcol_ref[...]
    lf_c = _log_sigmoid(gc)
    lf_r = _log_sigmoid(gr)
    ti = lax.broadcasted_iota(jnp.int32, (L, L), 0)
    si = lax.broadcasted_iota(jnp.int32, (L, L), 1)
    causal = si <= ti
    tril = causal.astype(BF16)
    triu = (ti <= si).astype(BF16)
    b_cols = sum(jnp.dot(tril, part, preferred_element_type=F32) for part in _split3(lf_c))
    b_rows = sum(jnp.dot(part, triu, preferred_element_type=F32) for part in _split3(lf_r))

    for h in range(ML_HEADS):
        q = qk[:, h * ML_QK_DIM:(h + 1) * ML_QK_DIM] * (ML_QK_DIM ** -0.5)
        k = qk[:, ML_QK_WIDTH + h * ML_QK_DIM:ML_QK_WIDTH + (h + 1) * ML_QK_DIM]
        v = mv_ref[:, h * ML_V_DIM:(h + 1) * ML_V_DIM]
        qb = q.astype(BF16)
        i_col = gc[:, h:h + 1]
        i_row = gr[h:h + 1, :]
        b_col = b_cols[:, ML_HEADS + h:ML_HEADS + h + 1]
        b_row = b_rows[ML_HEADS + h:ML_HEADS + h + 1, :]
        c_prev = c_sc[h]
        n_prev = n_sc[h]
        m_prev = m_sc[h]

        log_inter = b_col + m_prev
        dmat = jnp.where(causal, b_col - b_row + i_row, NEG_BIG)
        m_t = jnp.maximum(log_inter, jnp.max(dmat, axis=1, keepdims=True))
        inter_w = jnp.exp(log_inter - m_t)
        s_qk = lax.dot_general(qb, k.astype(BF16), (((1,), (1,)), ((), ())),
                               preferred_element_type=F32)
        p = jnp.exp(dmat - m_t) * s_qk
        num = (inter_w * jnp.dot(qb, c_prev.astype(BF16), preferred_element_type=F32)
               + jnp.dot(p.astype(BF16), v, preferred_element_type=F32))
        den = (inter_w * jnp.sum(q * n_prev, axis=1, keepdims=True)
               + jnp.sum(p, axis=1, keepdims=True))
        hh = num / jnp.maximum(jnp.abs(den), jnp.exp(-m_t))
        ms = jnp.mean(hh * hh, axis=1, keepdims=True)
        yh = hh * lax.rsqrt(ms + NORM_EPS) * og_ref[h:h + 1, :]
        gate = jax.nn.sigmoid(mo_ref[:, h * ML_V_DIM:(h + 1) * ML_V_DIM].astype(F32))
        out_ref[:, h * ML_V_DIM:(h + 1) * ML_V_DIM] = (yh * gate).astype(out_ref.dtype)

        a = b_col[L - 1:L, :]
        g_col = a - b_col + i_col
        g_max = jnp.max(g_col, axis=0, keepdims=True)
        kw = k * jnp.exp(g_col - g_max)
        c_loc = lax.dot_general(kw.astype(BF16), v, (((0,), (0,)), ((), ())),
                                preferred_element_type=F32)
        n_loc = jnp.sum(kw, axis=0, keepdims=True)
        m_new = jnp.maximum(a + m_prev, g_max)
        decay = jnp.exp(a + m_prev - m_new)
        scale = jnp.exp(g_max - m_new)
        c_sc[h] = decay * c_prev + scale * c_loc
        n_sc[h] = decay * n_prev + scale * n_loc
        m_sc[h] = m_new


def _mlstm(proj3, gates3, gates_t, conv_w, conv_b, bias_row, bias_col, out_g, L):
    B, S, _ = proj3.shape
    full = lambda shape: pl.BlockSpec(shape, lambda b, c: (0,) * len(shape))
    return pl.pallas_call(
        functools.partial(_mlstm_body, L=L),
        grid=(B, S // L),
        in_specs=[
            pl.BlockSpec((None, L, ML_QK_WIDTH), lambda b, c: (b, c, COL_ML_Q)),
            pl.BlockSpec((None, L, ML_QK_WIDTH), lambda b, c: (b, c, COL_ML_K)),
            pl.BlockSpec((None, L, ML_WIDTH), lambda b, c: (b, c, COL_ML_V)),
            pl.BlockSpec((None, L, ML_WIDTH), lambda b, c: (b, c, COL_ML_O)),
            pl.BlockSpec((None, L, LANES), lambda b, c: (b, c, 0)),
            pl.BlockSpec((None, N_GATES, L), lambda b, c: (b, 0, c)),
            full((CONV_WIDTH, 2 * ML_QK_WIDTH)),
            full((1, 2 * ML_QK_WIDTH)),
            full((1, LANES)),
            full((N_GATES, 1)),
            full((ML_HEADS, ML_V_DIM)),
        ],
        out_specs=pl.BlockSpec((None, L, ML_WIDTH), lambda b, c: (b, c, 0)),
        out_shape=jax.ShapeDtypeStruct((B, S, ML_WIDTH), BF16),
        scratch_shapes=[
            pltpu.VMEM((ML_HEADS, ML_QK_DIM, ML_V_DIM), F32),
            pltpu.VMEM((ML_HEADS, 1, ML_QK_DIM), F32),
            pltpu.VMEM((ML_HEADS, 1, 1), F32),
            pltpu.VMEM((L + 8, 2 * ML_QK_WIDTH), F32),
        ],
        compiler_params=_cparams(("parallel", "arbitrary")),
        name="mlstm",
    )(proj3, proj3, proj3, proj3, gates3, gates_t, conv_w, conv_b, bias_row, bias_col, out_g)


def _outproj_body(x_ref, d_ref, m_ref, wd_ref, wm_ref, o_ref):
    o_ref[...] = (x_ref[...]
                  + jnp.dot(d_ref[...], wd_ref[...], preferred_element_type=F32)
                  + jnp.dot(m_ref[...], wm_ref[...], preferred_element_type=F32))


def _outproj(x2, d2, m2, w_out, bm):
    T = x2.shape[0]
    return pl.pallas_call(
        _outproj_body,
        grid=(T // bm,),
        in_specs=[
            pl.BlockSpec((bm, D_MODEL), lambda m: (m, 0)),
            pl.BlockSpec((bm, DA_WIDTH), lambda m: (m, 0)),
            pl.BlockSpec((bm, ML_WIDTH), lambda m: (m, 0)),
            pl.BlockSpec((DA_WIDTH, D_MODEL), lambda m: (0, 0)),
            pl.BlockSpec((ML_WIDTH, D_MODEL), lambda m: (1, 0)),
        ],
        out_specs=pl.BlockSpec((bm, D_MODEL), lambda m: (m, 0)),
        out_shape=jax.ShapeDtypeStruct((T, D_MODEL), F32),
        compiler_params=_cparams(("parallel",)),
        name="outproj",
    )(x2, d2, m2, w_out, w_out)


def _mlp_body(x_ref, g_ref, wu_ref, wd_ref, o_ref, h_ref):
    @pl.when(pl.program_id(1) == 0)
    def _():
        x = x_ref[...]
        ms = jnp.mean(x * x, axis=-1, keepdims=True)
        h_ref[...] = (x * lax.rsqrt(ms + NORM_EPS) * g_ref[...]).astype(BF16)
        o_ref[...] = x

    u = jnp.dot(h_ref[...], wu_ref[...], preferred_element_type=F32)
    a = jnp.square(jnp.maximum(u, 0.0)).astype(BF16)
    o_ref[...] += jnp.dot(a, wd_ref[...], preferred_element_type=F32)


def _mlp(x1, g, w_up, w_down, bm, tf):
    T = x1.shape[0]
    return pl.pallas_call(
        _mlp_body,
        grid=(T // bm, D_FF // tf),
        in_specs=[
            pl.BlockSpec((bm, D_MODEL), lambda m, f: (m, 0)),
            pl.BlockSpec((1, D_MODEL), lambda m, f: (0, 0)),
            pl.BlockSpec((D_MODEL, tf), lambda m, f: (0, f)),
            pl.BlockSpec((tf, D_MODEL), lambda m, f: (f, 0)),
        ],
        out_specs=pl.BlockSpec((bm, D_MODEL), lambda m, f: (m, 0)),
        out_shape=jax.ShapeDtypeStruct((T, D_MODEL), F32),
        scratch_shapes=[pltpu.VMEM((bm, D_MODEL), BF16)],
        compiler_params=_cparams(("parallel", "arbitrary")),
        name="mlp",
    )(x1, g, w_up, w_down)


def _tiles(B, S):
    T = B * S
    return dict(
        bm_in=min(512, T), bn_in=1024,
        tp=min(512, S // 2),
        tq=min(1024, S),
        L=min(256, S),
        bm_out=min(512, T),
        bm_mlp=min(512, T), tf=1024,
    )


def kernel(x, norm1_g, w_in, ml_conv_w, ml_conv_b, ml_b_i, ml_b_f, ml_out_g, da_q_norm_g, da_k_norm_g, da_lambda_q1, da_lambda_k1, da_lambda_q2, da_lambda_k2, da_out_g, w_out, norm2_g, w_up, w_down):
    B, S, D = x.shape
    assert D == D_MODEL and norm1_g.shape[0] == 1, "single-layer kernel"
    t = _tiles(B, S)
    T = B * S
    x2 = x.reshape(T, D)

    w_in0 = w_in[0]
    w_main = w_in0[:, :D_MAIN].astype(BF16)
    w_gate = jnp.pad(w_in0[:, D_MAIN:], ((0, 0), (0, LANES - N_GATES))).astype(BF16)
    gq2 = jnp.tile(da_q_norm_g[0], 2).reshape(1, LANES) * (DA_QK_DIM ** -0.5 * math.log2(math.e))
    gk2 = jnp.tile(da_k_norm_g[0], 2).reshape(1, LANES)
    score_bound = DA_QK_DIM * jnp.max(jnp.abs(gq2)) * jnp.max(jnp.abs(gk2))
    bounded = (score_bound <= SCORE_BOUND).astype(jnp.int32).reshape(1)
    lam4 = jnp.stack([da_lambda_q1[0], da_lambda_k1[0], da_lambda_q2[0], da_lambda_k2[0]])
    bias8 = jnp.concatenate([ml_b_i[0], ml_b_f[0]])
    bias_row = jnp.pad(bias8, (0, LANES - N_GATES)).reshape(1, LANES)
    bias_col = bias8.reshape(N_GATES, 1)

    proj, gates = _inproj(x2, norm1_g, w_main, w_gate, t["bm_in"], t["bn_in"])
    proj3 = proj.reshape(B, S, D_MAIN)
    gates3 = gates.reshape(B, S, LANES)
    gates_t = jnp.transpose(gates3[:, :, :N_GATES], (0, 2, 1))

    qn, kn, vt = _prep(proj3, gq2, gk2, t["tp"])
    d_out = _attn(bounded, lam4, qn, kn, vt, da_out_g[0].reshape(DA_HEAD_DIM, 1), t["tq"], t["tp"])
    m_out = _mlstm(proj3, gates3, gates_t, ml_conv_w[0], ml_conv_b, bias_row, bias_col,
                   ml_out_g[0], t["L"])

    x1 = _outproj(x2, d_out.reshape(T, DA_WIDTH), m_out.reshape(T, ML_WIDTH),
                  w_out[0].astype(BF16), t["bm_out"])
    y = _mlp(x1, norm2_g, w_up[0].astype(BF16), w_down[0].astype(BF16), t["bm_mlp"], t["tf"])
    return y.reshape(B, S, D)
```

```python
import functools
import math

import jax
import jax.numpy as jnp
from jax import lax
from jax.experimental import pallas as pl
from jax.experimental.pallas import tpu as pltpu

F32 = jnp.float32
BF16 = jnp.bfloat16

D_MODEL = 2048
DA_HEADS = 8
DA_HEAD_DIM = 128
DA_QK_DIM = 64
DA_WIDTH = DA_HEADS * DA_HEAD_DIM
ML_HEADS = 4
ML_V_DIM = 256
ML_QK_DIM = 128
ML_WIDTH = ML_HEADS * ML_V_DIM
ML_QK_WIDTH = ML_HEADS * ML_QK_DIM
CONV_WIDTH = 4
D_FF = 4 * D_MODEL
D_MAIN = 3 * DA_WIDTH + 2 * ML_QK_WIDTH + 2 * ML_WIDTH
N_GATES = 2 * ML_HEADS
NORM_EPS = 1e-6
LAMBDA_INIT = 0.8 - 0.6 * math.exp(-0.3 * 0)
LANES = 128
STRIP = 256
NEG_BIG = -1e30
SCORE_BOUND = 60.0
VMEM_LIMIT = 56 * 1024 * 1024

COL_DA_Q, COL_DA_K, COL_DA_V = 0, 1, 2
COL_ML_Q, COL_ML_K = 6, 7
COL_ML_V, COL_ML_O = 4, 5


def _cparams(sem):
    return pltpu.CompilerParams(dimension_semantics=sem, vmem_limit_bytes=VMEM_LIMIT)


def _inproj_body(x_ref, g_ref, w_ref, wg_ref, o_ref, og_ref, h_ref):
    @pl.when(pl.program_id(1) == 0)
    def _():
        x = x_ref[...]
        ms = jnp.mean(x * x, axis=-1, keepdims=True)
        hb = (x * lax.rsqrt(ms + NORM_EPS) * g_ref[...]).astype(BF16)
        h_ref[...] = hb
        og_ref[...] = jnp.dot(hb, wg_ref[...], preferred_element_type=F32)

    o_ref[...] = jnp.dot(h_ref[...], w_ref[...], preferred_element_type=F32).astype(o_ref.dtype)


def _inproj(x2, g, w_main, w_gate, bm, bn):
    T = x2.shape[0]
    return pl.pallas_call(
        _inproj_body,
        grid=(T // bm, D_MAIN // bn),
        in_specs=[
            pl.BlockSpec((bm, D_MODEL), lambda m, n: (m, 0)),
            pl.BlockSpec((1, D_MODEL), lambda m, n: (0, 0)),
            pl.BlockSpec((None, D_MODEL, bn), lambda m, n: (n, 0, 0)),
            pl.BlockSpec((D_MODEL, LANES), lambda m, n: (0, 0)),
        ],
        out_specs=[
            pl.BlockSpec((bm, bn), lambda m, n: (m, n)),
            pl.BlockSpec((bm, LANES), lambda m, n: (m, 0)),
        ],
        out_shape=[
            jax.ShapeDtypeStruct((T, D_MAIN), BF16),
            jax.ShapeDtypeStruct((T, LANES), F32),
        ],
        scratch_shapes=[pltpu.VMEM((bm, D_MODEL), BF16)],
        compiler_params=_cparams(("parallel", "arbitrary")),
        name="inproj",
    )(x2, g, w_main, w_gate)


def _prep_body(q_ref, k_ref, v_ref, gq_ref, gk_ref, qt_ref, kn_ref, vt_ref):
    r = lax.broadcasted_iota(jnp.int32, (STRIP, STRIP), 0) // DA_QK_DIM
    c = lax.broadcasted_iota(jnp.int32, (STRIP, STRIP), 1) // DA_QK_DIM
    group = (r == c).astype(BF16)

    def norm(x, g):
        sq = x * x
        hi = sq.astype(BF16)
        lo = (sq - hi.astype(F32)).astype(BF16)
        ss = (jnp.dot(hi, group, preferred_element_type=F32)
              + jnp.dot(lo, group, preferred_element_type=F32))
        return x * lax.rsqrt(ss * (1.0 / DA_QK_DIM) + NORM_EPS) * g

    for hp in range(DA_WIDTH // STRIP):
        sl = slice(hp * STRIP, (hp + 1) * STRIP)
        qn = norm(q_ref[:, sl].astype(F32), gq_ref[...])
        kn_ref[:, sl] = norm(k_ref[:, sl].astype(F32), gk_ref[...]).astype(BF16)
        v = v_ref[:, sl].astype(F32)
        for j in range(STRIP // DA_HEAD_DIM):
            h = hp * (STRIP // DA_HEAD_DIM) + j
            hs = slice(j * DA_HEAD_DIM, (j + 1) * DA_HEAD_DIM)
            qt_ref[h] = qn[:, hs].T.astype(BF16)
            vt_ref[h] = v[:, hs].T.astype(BF16)


def _prep(proj3, gq2, gk2, tp):
    B, S, _ = proj3.shape
    nk = S // tp
    blk = lambda col: pl.BlockSpec((None, tp, DA_WIDTH), lambda b, i, col=col: (b, i, col))
    vec = pl.BlockSpec((1, STRIP), lambda b, i: (0, 0))
    transposed = pl.BlockSpec((None, DA_HEADS, None, DA_HEAD_DIM, tp), lambda b, i: (b, 0, i, 0, 0))
    return pl.pallas_call(
        _prep_body,
        grid=(B, nk),
        in_specs=[blk(COL_DA_Q), blk(COL_DA_K), blk(COL_DA_V), vec, vec],
        out_specs=[
            transposed,
            pl.BlockSpec((None, tp, DA_WIDTH), lambda b, i: (b, i, 0)),
            transposed,
        ],
        out_shape=[
            jax.ShapeDtypeStruct((B, DA_HEADS, nk, DA_HEAD_DIM, tp), BF16),
            jax.ShapeDtypeStruct((B, S, DA_WIDTH), BF16),
            jax.ShapeDtypeStruct((B, DA_HEADS, nk, DA_HEAD_DIM, tp), BF16),
        ],
        compiler_params=_cparams(("parallel", "parallel")),
        name="attn_prep",
    )(proj3, proj3, proj3, gq2, gk2)


def _attn_body(bounded_ref, lam_ref, q_ref, k_ref, vt_ref, g_ref, o_ref,
               qq_sc, s0_sc, s1_sc, p0_sc, p1_sc, a0_sc, a1_sc, c0_sc, c1_sc, m_sc, l_sc, acc_sc,
               *, tq, tk):
    qi = pl.program_id(2)
    s_sc, p_sc, a_sc, c_sc = (s0_sc, s1_sc), (p0_sc, p1_sc), (a0_sc, a1_sc), (c0_sc, c1_sc)

    row = lax.broadcasted_iota(jnp.int32, (DA_HEAD_DIM, tk), 0)
    for c in range(tq // tk):
        qt = q_ref[c]
        zero = jnp.zeros_like(qt)
        qq_sc[:, c * tk:(c + 1) * tk] = jnp.where(row < DA_QK_DIM, qt, zero)
        qq_sc[:, tq + c * tk:tq + (c + 1) * tk] = jnp.where(row >= DA_QK_DIM, qt, zero)

    l_sc[...] = jnp.zeros(l_sc.shape, F32)
    acc_sc[...] = jnp.zeros(acc_sc.shape, F32)

    strips = [slice(c * STRIP, (c + 1) * STRIP) for c in range(2 * tq // STRIP)]

    def key_chunk(t):
        return k_ref[pl.ds(pl.multiple_of(t * tk, tk), tk), :]

    def causal(diag, sl):
        q_lo = sl.start % tq
        if q_lo >= (diag + 1) * tk - 1:
            return None
        kpos = diag * tk + lax.broadcasted_iota(jnp.int32, (tk, STRIP), 0)
        qpos = q_lo + lax.broadcasted_iota(jnp.int32, (tk, STRIP), 1)
        return kpos <= qpos

    def visible(diag, sl):
        return (sl.start % tq) + STRIP - 1 >= diag * tk

    def probs(t, par, diag=None):
        kj = key_chunk(t)
        for sl in strips:
            if diag is not None and not visible(diag, sl):
                continue
            p = jnp.exp2(jnp.dot(kj, qq_sc[:, sl], preferred_element_type=F32))
            mask = None if diag is None else causal(diag, sl)
            if mask is not None:
                p = jnp.where(mask, p, 0.0)
            l_sc[:, sl] += jnp.sum(p, axis=0, keepdims=True)
            p_sc[par][:, sl] = p.astype(BF16)

    def values_plain(t, par, diag=None):
        vt = vt_ref[t]
        for sl in strips:
            if diag is not None and not visible(diag, sl):
                continue
            acc_sc[:, sl] += jnp.dot(vt, p_sc[par][:, sl], preferred_element_type=F32)

    def bounded_path():
        d0 = 2 * qi
        probs(d0, 0, diag=0)
        probs(d0 + 1, 1, diag=1)
        values_plain(d0, 0, diag=0)

        @pl.when(qi == 0)
        def _():
            values_plain(d0 + 1, 1, diag=1)

        @pl.when(qi > 0)
        def _():
            probs(0, 0)
            values_plain(d0 + 1, 1, diag=1)

            def pair(t):
                probs(t + 1, 1)
                values_plain(t, 0)
                probs(t + 2, 0)
                values_plain(t + 1, 1)

            def two_pairs(i, carry):
                pair(4 * i)
                pair(4 * i + 2)
                return carry

            n_pairs = qi - 1
            lax.fori_loop(0, lax.shift_right_logical(n_pairs, 1), two_pairs, 0)

            @pl.when((n_pairs & 1) == 1)
            def _():
                pair(2 * n_pairs - 2)

            t = 2 * qi - 2
            probs(t + 1, 1)
            values_plain(t, 0)
            values_plain(t + 1, 1)

    def scores(t, par):
        kj = key_chunk(t)
        for sl in strips:
            s = jnp.dot(kj, qq_sc[:, sl], preferred_element_type=F32)
            s_sc[par][:, sl] = s
            c_sc[par][:, sl] = jnp.max(s, axis=0, keepdims=True)

    def softmax(par, diag):
        for sl in strips:
            s = s_sc[par][:, sl]
            mask = None if diag is None else causal(diag, sl)
            if mask is None:
                cmax = c_sc[par][:, sl]
            else:
                s = jnp.where(mask, s, NEG_BIG)
                cmax = jnp.max(s, axis=0, keepdims=True)
            m_old = m_sc[:, sl]
            m_new = jnp.maximum(m_old, cmax)
            p = jnp.exp2(s - m_new)
            alpha = jnp.exp2(m_old - m_new)
            l_sc[:, sl] = alpha * l_sc[:, sl] + jnp.sum(p, axis=0, keepdims=True)
            m_sc[:, sl] = m_new
            a_sc[par][:, sl] = alpha
            p_sc[par][:, sl] = p.astype(BF16)

    def values(t, par):
        vt = vt_ref[t]
        for sl in strips:
            acc_sc[:, sl] = a_sc[par][:, sl] * acc_sc[:, sl] + jnp.dot(
                vt, p_sc[par][:, sl], preferred_element_type=F32)

    def online_path():
        m_sc[...] = jnp.full(m_sc.shape, NEG_BIG, F32)
        p1_sc[...] = jnp.zeros(p1_sc.shape, BF16)
        a1_sc[...] = jnp.ones(a1_sc.shape, F32)
        scores(0, 0)

        def pair(i, carry):
            t = 2 * i
            scores(t + 1, 1)
            softmax(0, None)
            values(jnp.maximum(t - 1, 0), 1)
            scores(t + 2, 0)
            softmax(1, None)
            values(t, 0)
            return carry

        lax.fori_loop(0, qi, pair, 0)
        t = 2 * qi
        scores(t + 1, 1)
        softmax(0, 0)
        values(jnp.maximum(t - 1, 0), 1)
        softmax(1, 1)
        values(t, 0)
        values(t + 1, 1)

    pl.when(bounded_ref[0] == 1)(bounded_path)
    pl.when(bounded_ref[0] != 1)(online_path)

    lv = lam_ref[...]
    lam = (jnp.exp(jnp.sum(lv[0:1] * lv[1:2], axis=-1, keepdims=True))
           - jnp.exp(jnp.sum(lv[2:3] * lv[3:4], axis=-1, keepdims=True)) + LAMBDA_INIT)
    acc = acc_sc[...]
    l = l_sc[...]
    o = acc[:, :tq] / l[:, :tq] - lam * (acc[:, tq:] / l[:, tq:])
    ms = jnp.mean(o * o, axis=0, keepdims=True)
    y = o * lax.rsqrt(ms + NORM_EPS) * g_ref[...] * (1.0 - LAMBDA_INIT)
    o_ref[...] = y.T.astype(o_ref.dtype)


def _attn(bounded, lam4, qt, kn, vt, g_col, tq, tk):
    B, S, _ = kn.shape
    nk = S // tk
    assert tq == 2 * tk, "a query block spans exactly two key chunks"
    grid_spec = pltpu.PrefetchScalarGridSpec(
        num_scalar_prefetch=1,
        grid=(B, DA_HEADS, S // tq),
        in_specs=[
            pl.BlockSpec((4, DA_QK_DIM), lambda b, h, i, f: (0, 0)),
            pl.BlockSpec((None, None, tq // tk, DA_HEAD_DIM, tk), lambda b, h, i, f: (b, h, i, 0, 0)),
            pl.BlockSpec((None, S, DA_HEAD_DIM), lambda b, h, i, f: (b, 0, h)),
            pl.BlockSpec((None, None, nk, DA_HEAD_DIM, tk), lambda b, h, i, f: (b, h, 0, 0, 0)),
            pl.BlockSpec((DA_HEAD_DIM, 1), lambda b, h, i, f: (0, 0)),
        ],
        out_specs=pl.BlockSpec((None, tq, DA_HEAD_DIM), lambda b, h, i, f: (b, i, h)),
        scratch_shapes=[
            pltpu.VMEM((DA_HEAD_DIM, 2 * tq), BF16),
            pltpu.VMEM((tk, 2 * tq), F32),
            pltpu.VMEM((tk, 2 * tq), F32),
            pltpu.VMEM((tk, 2 * tq), BF16),
            pltpu.VMEM((tk, 2 * tq), BF16),
            pltpu.VMEM((1, 2 * tq), F32),
            pltpu.VMEM((1, 2 * tq), F32),
            pltpu.VMEM((1, 2 * tq), F32),
            pltpu.VMEM((1, 2 * tq), F32),
            pltpu.VMEM((1, 2 * tq), F32),
            pltpu.VMEM((1, 2 * tq), F32),
            pltpu.VMEM((DA_HEAD_DIM, 2 * tq), F32),
        ],
    )
    return pl.pallas_call(
        functools.partial(_attn_body, tq=tq, tk=tk),
        grid_spec=grid_spec,
        out_shape=jax.ShapeDtypeStruct((B, S, DA_WIDTH), BF16),
        compiler_params=_cparams(("parallel", "parallel", "arbitrary")),
        name="diff_attn",
    )(bounded, lam4, qt, kn, vt, g_col)


def _log_sigmoid(x):
    return jnp.minimum(x, 0.0) - jnp.log1p(jnp.exp(-jnp.abs(x)))


def _split3(x):
    hi = x.astype(BF16)
    r1 = x - hi.astype(F32)
    mid = r1.astype(BF16)
    lo = (r1 - mid.astype(F32)).astype(BF16)
    return hi, mid, lo


def _mlstm_body(mq_ref, mk_ref, mv_ref, mo_ref, gc_ref, gr_ref, cw_ref, cb_ref, brow_ref, bcol_ref,
                og_ref, out_ref, c_sc, n_sc, m_sc, ext_sc, *, L):
    @pl.when(pl.program_id(1) == 0)
    def _():
        c_sc[...] = jnp.zeros(c_sc.shape, F32)
        n_sc[...] = jnp.zeros(n_sc.shape, F32)
        m_sc[...] = jnp.zeros(m_sc.shape, F32)
        ext_sc[0:8, :] = jnp.zeros((8, 2 * ML_QK_WIDTH), F32)

    x = jnp.concatenate([mq_ref[...].astype(F32), mk_ref[...].astype(F32)], axis=1)
    ext_sc[8:8 + L, :] = x
    cw = cw_ref[...]
    y = cb_ref[...] + cw[0:1] * ext_sc[5:5 + L, :]
    y = y + cw[1:2] * ext_sc[6:6 + L, :]
    y = y + cw[2:3] * ext_sc[7:7 + L, :]
    y = y + cw[3:4] * x
    ext_sc[0:8, :] = x[L - 8:L, :]
    qk = y * jax.nn.sigmoid(y)

    gc = gc_ref[...] + brow_ref[...]
    gr = gr_ref[...] + bcol_ref[...]
    lf_c = _log_sigmoid(gc)
    lf_r = _log_sigmoid(gr)
    ti = lax.broadcasted_iota(jnp.int32, (L, L), 0)
    si = lax.broadcasted_iota(jnp.int32, (L, L), 1)
    causal = si <= ti
    tril = causal.astype(BF16)
    triu = (ti <= si).astype(BF16)
    b_cols = sum(jnp.dot(tril, part, preferred_element_type=F32) for part in _split3(lf_c))
    b_rows = sum(jnp.dot(part, triu, preferred_element_type=F32) for part in _split3(lf_r))

    for h in range(ML_HEADS):
        q = qk[:, h * ML_QK_DIM:(h + 1) * ML_QK_DIM] * (ML_QK_DIM ** -0.5)
        k = qk[:, ML_QK_WIDTH + h * ML_QK_DIM:ML_QK_WIDTH + (h + 1) * ML_QK_DIM]
        v = mv_ref[:, h * ML_V_DIM:(h + 1) * ML_V_DIM]
        qb = q.astype(BF16)
        i_col = gc[:, h:h + 1]
        i_row = gr[h:h + 1, :]
        b_col = b_cols[:, ML_HEADS + h:ML_HEADS + h + 1]
        b_row = b_rows[ML_HEADS + h:ML_HEADS + h + 1, :]
        c_prev = c_sc[h]
        n_prev = n_sc[h]
        m_prev = m_sc[h]

        log_inter = b_col + m_prev
        dmat = jnp.where(causal, b_col - b_row + i_row, NEG_BIG)
        m_t = jnp.maximum(log_inter, jnp.max(dmat, axis=1, keepdims=True))
        inter_w = jnp.exp(log_inter - m_t)
        s_qk = lax.dot_general(qb, k.astype(BF16), (((1,), (1,)), ((), ())),
                               preferred_element_type=F32)
        p = jnp.exp(dmat - m_t) * s_qk
        num = (inter_w * jnp.dot(qb, c_prev.astype(BF16), preferred_element_type=F32)
               + jnp.dot(p.astype(BF16), v, preferred_element_type=F32))
        den = (inter_w * jnp.sum(q * n_prev, axis=1, keepdims=True)
               + jnp.sum(p, axis=1, keepdims=True))
        hh = num / jnp.maximum(jnp.abs(den), jnp.exp(-m_t))
        ms = jnp.mean(hh * hh, axis=1, keepdims=True)
        yh = hh * lax.rsqrt(ms + NORM_EPS) * og_ref[h:h + 1, :]
        gate = jax.nn.sigmoid(mo_ref[:, h * ML_V_DIM:(h + 1) * ML_V_DIM].astype(F32))
        out_ref[:, h * ML_V_DIM:(h + 1) * ML_V_DIM] = (yh * gate).astype(out_ref.dtype)

        a = b_col[L - 1:L, :]
        g_col = a - b_col + i_col
        g_max = jnp.max(g_col, axis=0, keepdims=True)
        kw = k * jnp.exp(g_col - g_max)
        c_loc = lax.dot_general(kw.astype(BF16), v, (((0,), (0,)), ((), ())),
                                preferred_element_type=F32)
        n_loc = jnp.sum(kw, axis=0, keepdims=True)
        m_new = jnp.maximum(a + m_prev, g_max)
        decay = jnp.exp(a + m_prev - m_new)
        scale = jnp.exp(g_max - m_new)
        c_sc[h] = decay * c_prev + scale * c_loc
        n_sc[h] = decay * n_prev + scale * n_loc
        m_sc[h] = m_new


def _mlstm(proj3, gates3, gates_t, conv_w, conv_b, bias_row, bias_col, out_g, L):
    B, S, _ = proj3.shape
    full = lambda shape: pl.BlockSpec(shape, lambda b, c: (0,) * len(shape))
    return pl.pallas_call(
        functools.partial(_mlstm_body, L=L),
        grid=(B, S // L),
        in_specs=[
            pl.BlockSpec((None, L, ML_QK_WIDTH), lambda b, c: (b, c, COL_ML_Q)),
            pl.BlockSpec((None, L, ML_QK_WIDTH), lambda b, c: (b, c, COL_ML_K)),
            pl.BlockSpec((None, L, ML_WIDTH), lambda b, c: (b, c, COL_ML_V)),
            pl.BlockSpec((None, L, ML_WIDTH), lambda b, c: (b, c, COL_ML_O)),
            pl.BlockSpec((None, L, LANES), lambda b, c: (b, c, 0)),
            pl.BlockSpec((None, N_GATES, L), lambda b, c: (b, 0, c)),
            full((CONV_WIDTH, 2 * ML_QK_WIDTH)),
            full((1, 2 * ML_QK_WIDTH)),
            full((1, LANES)),
            full((N_GATES, 1)),
            full((ML_HEADS, ML_V_DIM)),
        ],
        out_specs=pl.BlockSpec((None, L, ML_WIDTH), lambda b, c: (b, c, 0)),
        out_shape=jax.ShapeDtypeStruct((B, S, ML_WIDTH), BF16),
        scratch_shapes=[
            pltpu.VMEM((ML_HEADS, ML_QK_DIM, ML_V_DIM), F32),
            pltpu.VMEM((ML_HEADS, 1, ML_QK_DIM), F32),
            pltpu.VMEM((ML_HEADS, 1, 1), F32),
            pltpu.VMEM((L + 8, 2 * ML_QK_WIDTH), F32),
        ],
        compiler_params=_cparams(("parallel", "arbitrary")),
        name="mlstm",
    )(proj3, proj3, proj3, proj3, gates3, gates_t, conv_w, conv_b, bias_row, bias_col, out_g)


def _outproj_body(x_ref, d_ref, m_ref, wd_ref, wm_ref, o_ref):
    o_ref[...] = (x_ref[...]
                  + jnp.dot(d_ref[...], wd_ref[...], preferred_element_type=F32)
                  + jnp.dot(m_ref[...], wm_ref[...], preferred_element_type=F32))


def _outproj(x2, d2, m2, w_out, bm):
    T = x2.shape[0]
    return pl.pallas_call(
        _outproj_body,
        grid=(T // bm,),
        in_specs=[
            pl.BlockSpec((bm, D_MODEL), lambda m: (m, 0)),
            pl.BlockSpec((bm, DA_WIDTH), lambda m: (m, 0)),
            pl.BlockSpec((bm, ML_WIDTH), lambda m: (m, 0)),
            pl.BlockSpec((DA_WIDTH, D_MODEL), lambda m: (0, 0)),
            pl.BlockSpec((ML_WIDTH, D_MODEL), lambda m: (1, 0)),
        ],
        out_specs=pl.BlockSpec((bm, D_MODEL), lambda m: (m, 0)),
        out_shape=jax.ShapeDtypeStruct((T, D_MODEL), F32),
        compiler_params=_cparams(("parallel",)),
        name="outproj",
    )(x2, d2, m2, w_out, w_out)


def _mlp_body(x_ref, g_ref, wu_ref, wd_ref, o_ref, h_ref):
    @pl.when(pl.program_id(1) == 0)
    def _():
        x = x_ref[...]
        ms = jnp.mean(x * x, axis=-1, keepdims=True)
        h_ref[...] = (x * lax.rsqrt(ms + NORM_EPS) * g_ref[...]).astype(BF16)
        o_ref[...] = x

    u = jnp.dot(h_ref[...], wu_ref[...], preferred_element_type=F32)
    a = jnp.square(jnp.maximum(u, 0.0)).astype(BF16)
    o_ref[...] += jnp.dot(a, wd_ref[...], preferred_element_type=F32)


def _mlp(x1, g, w_up, w_down, bm, tf):
    T = x1.shape[0]
    return pl.pallas_call(
        _mlp_body,
        grid=(T // bm, D_FF // tf),
        in_specs=[
            pl.BlockSpec((bm, D_MODEL), lambda m, f: (m, 0)),
            pl.BlockSpec((1, D_MODEL), lambda m, f: (0, 0)),
            pl.BlockSpec((D_MODEL, tf), lambda m, f: (0, f)),
            pl.BlockSpec((tf, D_MODEL), lambda m, f: (f, 0)),
        ],
        out_specs=pl.BlockSpec((bm, D_MODEL), lambda m, f: (m, 0)),
        out_shape=jax.ShapeDtypeStruct((T, D_MODEL), F32),
        scratch_shapes=[pltpu.VMEM((bm, D_MODEL), BF16)],
        compiler_params=_cparams(("parallel", "arbitrary")),
        name="mlp",
    )(x1, g, w_up, w_down)


def _tiles(B, S):
    T = B * S
    return dict(
        bm_in=min(1024, T), bn_in=1024,
        tp=min(512, S // 2),
        tq=min(1024, S),
        L=min(128, S),
        bm_out=min(512, T),
        bm_mlp=min(512, T), tf=1024,
    )


def kernel(x, norm1_g, w_in, ml_conv_w, ml_conv_b, ml_b_i, ml_b_f, ml_out_g, da_q_norm_g, da_k_norm_g, da_lambda_q1, da_lambda_k1, da_lambda_q2, da_lambda_k2, da_out_g, w_out, norm2_g, w_up, w_down):
    B, S, D = x.shape
    assert D == D_MODEL and norm1_g.shape[0] == 1, "single-layer kernel"
    t = _tiles(B, S)
    T = B * S
    x2 = x.reshape(T, D)

    w_in0 = w_in[0]
    w_main = jnp.transpose(w_in0[:, :D_MAIN].reshape(D_MODEL, D_MAIN // t["bn_in"], t["bn_in"]),
                           (1, 0, 2)).astype(BF16)
    w_gate = jnp.pad(w_in0[:, D_MAIN:], ((0, 0), (0, LANES - N_GATES))).astype(BF16)
    reps = STRIP // DA_QK_DIM
    gq2 = jnp.tile(da_q_norm_g[0], reps).reshape(1, STRIP) * (DA_QK_DIM ** -0.5 * math.log2(math.e))
    gk2 = jnp.tile(da_k_norm_g[0], reps).reshape(1, STRIP)
    score_bound = DA_QK_DIM * jnp.max(jnp.abs(gq2)) * jnp.max(jnp.abs(gk2))
    bounded = (score_bound <= SCORE_BOUND).astype(jnp.int32).reshape(1)
    lam4 = jnp.stack([da_lambda_q1[0], da_lambda_k1[0], da_lambda_q2[0], da_lambda_k2[0]])
    bias8 = jnp.concatenate([ml_b_i[0], ml_b_f[0]])
    bias_row = jnp.pad(bias8, (0, LANES - N_GATES)).reshape(1, LANES)
    bias_col = bias8.reshape(N_GATES, 1)

    proj, gates = _inproj(x2, norm1_g, w_main, w_gate, t["bm_in"], t["bn_in"])
    proj3 = proj.reshape(B, S, D_MAIN)
    gates3 = gates.reshape(B, S, LANES)
    gates_t = jnp.transpose(gates3[:, :, :N_GATES], (0, 2, 1))

    qt, kn, vt = _prep(proj3, gq2, gk2, t["tp"])
    d_out = _attn(bounded, lam4, qt, kn, vt, da_out_g[0].reshape(DA_HEAD_DIM, 1), t["tq"], t["tp"])
    m_out = _mlstm(proj3, gates3, gates_t, ml_conv_w[0], ml_conv_b, bias_row, bias_col,
                   ml_out_g[0], t["L"])

    x1 = _outproj(x2, d_out.reshape(T, DA_WIDTH), m_out.reshape(T, ML_WIDTH),
                  w_out[0].astype(BF16), t["bm_out"])
    y = _mlp(x1, norm2_g, w_up[0].astype(BF16), w_down[0].astype(BF16), t["bm_mlp"], t["tf"])
    return y.reshape(B, S, D)
```

```python
import functools
import math

import jax
import jax.numpy as jnp
from jax import lax
from jax.experimental import pallas as pl
from jax.experimental.pallas import tpu as pltpu

F32 = jnp.float32
BF16 = jnp.bfloat16

D_MODEL = 2048
DA_HEADS = 8
DA_HEAD_DIM = 128
DA_QK_DIM = 64
DA_WIDTH = DA_HEADS * DA_HEAD_DIM
ML_HEADS = 4
ML_V_DIM = 256
ML_QK_DIM = 128
ML_WIDTH = ML_HEADS * ML_V_DIM
ML_QK_WIDTH = ML_HEADS * ML_QK_DIM
CONV_WIDTH = 4
D_FF = 4 * D_MODEL
D_MAIN = 3 * DA_WIDTH + 2 * ML_QK_WIDTH + 2 * ML_WIDTH
N_GATES = 2 * ML_HEADS
NORM_EPS = 1e-6
LAMBDA_INIT = 0.8 - 0.6 * math.exp(-0.3 * 0)
LANES = 128
STRIP = 256
NEG_BIG = -1e30
SCORE_BOUND = 60.0
VMEM_LIMIT = 56 * 1024 * 1024

COL_DA_Q, COL_DA_K, COL_DA_V = 0, 1, 2
COL_ML_Q, COL_ML_K = 6, 7
COL_ML_V, COL_ML_O = 4, 5


def _cparams(sem):
    return pltpu.CompilerParams(dimension_semantics=sem, vmem_limit_bytes=VMEM_LIMIT)


def _inproj_body(x_ref, g_ref, w_ref, wg_ref, o_ref, og_ref, h_ref):
    @pl.when(pl.program_id(1) == 0)
    def _():
        x = x_ref[...]
        ms = jnp.mean(x * x, axis=-1, keepdims=True)
        hb = (x * lax.rsqrt(ms + NORM_EPS) * g_ref[...]).astype(BF16)
        h_ref[...] = hb
        og_ref[...] = jnp.dot(hb, wg_ref[...], preferred_element_type=F32)

    o_ref[...] = jnp.dot(h_ref[...], w_ref[...], preferred_element_type=F32).astype(o_ref.dtype)


def _inproj(x2, g, w_main, w_gate, bm, bn):
    T = x2.shape[0]
    return pl.pallas_call(
        _inproj_body,
        grid=(T // bm, D_MAIN // bn),
        in_specs=[
            pl.BlockSpec((bm, D_MODEL), lambda m, n: (m, 0)),
            pl.BlockSpec((1, D_MODEL), lambda m, n: (0, 0)),
            pl.BlockSpec((None, D_MODEL, bn), lambda m, n: (n, 0, 0)),
            pl.BlockSpec((D_MODEL, LANES), lambda m, n: (0, 0)),
        ],
        out_specs=[
            pl.BlockSpec((bm, bn), lambda m, n: (m, n)),
            pl.BlockSpec((bm, LANES), lambda m, n: (m, 0)),
        ],
        out_shape=[
            jax.ShapeDtypeStruct((T, D_MAIN), BF16),
            jax.ShapeDtypeStruct((T, LANES), F32),
        ],
        scratch_shapes=[pltpu.VMEM((bm, D_MODEL), BF16)],
        compiler_params=_cparams(("parallel", "arbitrary")),
        name="inproj",
    )(x2, g, w_main, w_gate)


def _prep_body(q_ref, k_ref, v_ref, gq_ref, gk_ref, qt_ref, kn_ref, vt_ref):
    r = lax.broadcasted_iota(jnp.int32, (STRIP, STRIP), 0) // DA_QK_DIM
    c = lax.broadcasted_iota(jnp.int32, (STRIP, STRIP), 1) // DA_QK_DIM
    group = jnp.where(r == c, 1.0 / DA_QK_DIM, 0.0).astype(BF16)

    def norm(x, g):
        ms = jnp.dot((x * x).astype(BF16), group, preferred_element_type=F32)
        return x * lax.rsqrt(ms + NORM_EPS) * g

    for hp in range(DA_WIDTH // STRIP):
        sl = slice(hp * STRIP, (hp + 1) * STRIP)
        qn = norm(q_ref[:, sl].astype(F32), gq_ref[...])
        kn_ref[:, sl] = norm(k_ref[:, sl].astype(F32), gk_ref[...]).astype(BF16)
        v = v_ref[:, sl].astype(F32)
        for j in range(STRIP // DA_HEAD_DIM):
            h = hp * (STRIP // DA_HEAD_DIM) + j
            hs = slice(j * DA_HEAD_DIM, (j + 1) * DA_HEAD_DIM)
            qt_ref[h] = qn[:, hs].T.astype(BF16)
            vt_ref[h] = v[:, hs].T.astype(BF16)


def _prep(proj3, gq2, gk2, tp):
    B, S, _ = proj3.shape
    nk = S // tp
    blk = lambda col: pl.BlockSpec((None, tp, DA_WIDTH), lambda b, i, col=col: (b, i, col))
    vec = pl.BlockSpec((1, STRIP), lambda b, i: (0, 0))
    transposed = pl.BlockSpec((None, DA_HEADS, None, DA_HEAD_DIM, tp), lambda b, i: (b, 0, i, 0, 0))
    return pl.pallas_call(
        _prep_body,
        grid=(B, nk),
        in_specs=[blk(COL_DA_Q), blk(COL_DA_K), blk(COL_DA_V), vec, vec],
        out_specs=[
            transposed,
            pl.BlockSpec((None, tp, DA_WIDTH), lambda b, i: (b, i, 0)),
            transposed,
        ],
        out_shape=[
            jax.ShapeDtypeStruct((B, DA_HEADS, nk, DA_HEAD_DIM, tp), BF16),
            jax.ShapeDtypeStruct((B, S, DA_WIDTH), BF16),
            jax.ShapeDtypeStruct((B, DA_HEADS, nk, DA_HEAD_DIM, tp), BF16),
        ],
        compiler_params=_cparams(("parallel", "parallel")),
        name="attn_prep",
    )(proj3, proj3, proj3, gq2, gk2)


def _log_sigmoid(x):
    return jnp.minimum(x, 0.0) - jnp.log1p(jnp.exp(-jnp.abs(x)))


def _split3(x):
    hi = x.astype(BF16)
    r1 = x - hi.astype(F32)
    mid = r1.astype(BF16)
    lo = (r1 - mid.astype(F32)).astype(BF16)
    return hi, mid, lo


def _mlstm_reset(c_sc, n_sc, m_sc, ext_sc):
    c_sc[...] = jnp.zeros(c_sc.shape, F32)
    n_sc[...] = jnp.zeros(n_sc.shape, F32)
    m_sc[...] = jnp.zeros(m_sc.shape, F32)
    ext_sc[0:8, :] = jnp.zeros((8, 2 * ML_QK_WIDTH), F32)


def _mlstm_stages(mq_ref, mk_ref, mv_ref, mo_ref, gc_ref, gr_ref, cw_ref, cb_ref, brow_ref, bcol_ref,
                  og_ref, out_ref, c_sc, n_sc, m_sc, ext_sc, *, L):
    qk = []
    for src_ref in (mq_ref, mk_ref):
        for c in range(ML_QK_WIDTH // STRIP):
            sl = slice(c * STRIP, (c + 1) * STRIP)
            dst = slice(len(qk) * STRIP, (len(qk) + 1) * STRIP)
            x = src_ref[:, sl].astype(F32)
            ext_sc[8:8 + L, dst] = x
            y = cb_ref[:, dst] + cw_ref[0:1, dst] * ext_sc[5:5 + L, dst]
            y = y + cw_ref[1:2, dst] * ext_sc[6:6 + L, dst]
            y = y + cw_ref[2:3, dst] * ext_sc[7:7 + L, dst]
            y = y + cw_ref[3:4, dst] * x
            ext_sc[0:8, dst] = x[L - 8:L, :]
            qk.append(y * jax.nn.sigmoid(y))
            yield
    per_strip = STRIP // ML_QK_DIM
    head_cols = lambda j: qk[j // per_strip][:, (j % per_strip) * ML_QK_DIM:(j % per_strip + 1) * ML_QK_DIM]

    gc = gc_ref[...] + brow_ref[...]
    gr = gr_ref[...] + bcol_ref[...]
    lf_c = _log_sigmoid(gc)
    lf_r = _log_sigmoid(gr)
    ti = lax.broadcasted_iota(jnp.int32, (L, L), 0)
    si = lax.broadcasted_iota(jnp.int32, (L, L), 1)
    causal = si <= ti
    tril = causal.astype(BF16)
    triu = (ti <= si).astype(BF16)
    b_cols = sum(jnp.dot(tril, part, preferred_element_type=F32) for part in _split3(lf_c))
    b_rows = sum(jnp.dot(part, triu, preferred_element_type=F32) for part in _split3(lf_r))
    yield

    heads = range(ML_HEADS)
    q = [head_cols(h) * (ML_QK_DIM ** -0.5) for h in heads]
    k = [head_cols(ML_HEADS + h) for h in heads]
    v = [mv_ref[:, h * ML_V_DIM:(h + 1) * ML_V_DIM] for h in heads]
    i_col = [gc[:, h:h + 1] for h in heads]
    i_row = [gr[h:h + 1, :] for h in heads]
    b_col = [b_cols[:, ML_HEADS + h:ML_HEADS + h + 1] for h in heads]
    b_row = [b_rows[ML_HEADS + h:ML_HEADS + h + 1, :] for h in heads]
    num, den, m_t = {}, {}, {}

    for h in heads:
        m_prev = m_sc[h]
        log_inter = b_col[h] + m_prev
        dmat = jnp.where(causal, b_col[h] - b_row[h] + i_row[h], NEG_BIG)
        m_t[h] = jnp.maximum(log_inter, jnp.max(dmat, axis=1, keepdims=True))
        inter_w = jnp.exp(log_inter - m_t[h])
        qb = q[h].astype(BF16)
        s_qk = lax.dot_general(qb, k[h].astype(BF16), (((1,), (1,)), ((), ())),
                               preferred_element_type=F32)
        p = jnp.exp(dmat - m_t[h]) * s_qk
        num[h] = (inter_w * jnp.dot(qb, c_sc[h].astype(BF16), preferred_element_type=F32)
                  + jnp.dot(p.astype(BF16), v[h], preferred_element_type=F32))
        den[h] = (inter_w * jnp.sum(q[h] * n_sc[h], axis=1, keepdims=True)
                  + jnp.sum(p, axis=1, keepdims=True))
        yield

    for h in heads:
        hh = num[h] / jnp.maximum(jnp.abs(den[h]), jnp.exp(-m_t[h]))
        ms = jnp.mean(hh * hh, axis=1, keepdims=True)
        yh = hh * lax.rsqrt(ms + NORM_EPS) * og_ref[h:h + 1, :]
        gate = jax.nn.sigmoid(mo_ref[:, h * ML_V_DIM:(h + 1) * ML_V_DIM].astype(F32))
        out_ref[:, h * ML_V_DIM:(h + 1) * ML_V_DIM] = (yh * gate).astype(out_ref.dtype)
        yield

    for h in heads:
        m_prev = m_sc[h]
        a = b_col[h][L - 1:L, :]
        g_col = a - b_col[h] + i_col[h]
        g_max = jnp.max(g_col, axis=0, keepdims=True)
        kw = k[h] * jnp.exp(g_col - g_max)
        c_loc = lax.dot_general(kw.astype(BF16), v[h], (((0,), (0,)), ((), ())),
                                preferred_element_type=F32)
        n_loc = jnp.sum(kw, axis=0, keepdims=True)
        m_new = jnp.maximum(a + m_prev, g_max)
        decay = jnp.exp(a + m_prev - m_new)
        scale = jnp.exp(g_max - m_new)
        c_sc[h] = decay * c_sc[h] + scale * c_loc
        n_sc[h] = decay * n_sc[h] + scale * n_loc
        m_sc[h] = m_new
        yield


def _interleave(*stage_generators):
    live = list(stage_generators)
    while live:
        live = [g for g in live if next(g, StopIteration) is not StopIteration]


N_ATTN_IN, N_MLSTM_IN, N_ATTN_SCRATCH = 5, 11, 12


def _attn_body(bounded_ref, *refs, tq, tk):
    lam_ref, q_ref, k_ref, vt_ref, g_ref = refs[:N_ATTN_IN]
    ml_in = refs[N_ATTN_IN:N_ATTN_IN + N_MLSTM_IN]
    o_ref, ml_out = refs[N_ATTN_IN + N_MLSTM_IN:N_ATTN_IN + N_MLSTM_IN + 2]
    scratch = refs[N_ATTN_IN + N_MLSTM_IN + 2:]
    (qq_sc, s0_sc, s1_sc, p0_sc, p1_sc, a0_sc, a1_sc, c0_sc, c1_sc, m_sc, l_sc,
     acc_sc) = scratch[:N_ATTN_SCRATCH]
    ml_state = scratch[N_ATTN_SCRATCH:]
    s_sc, p_sc, a_sc, c_sc = (s0_sc, s1_sc), (p0_sc, p1_sc), (a0_sc, a1_sc), (c0_sc, c1_sc)
    qi = pl.program_id(2)

    @pl.when((pl.program_id(1) == 0) & (qi == 0))
    def _():
        _mlstm_reset(*ml_state)

    def mlstm_stages():
        return _mlstm_stages(*ml_in, ml_out, *ml_state, L=tq // DA_HEADS)

    row = lax.broadcasted_iota(jnp.int32, (DA_HEAD_DIM, tk), 0)
    for c in range(tq // tk):
        qt = q_ref[c]
        zero = jnp.zeros_like(qt)
        qq_sc[:, c * tk:(c + 1) * tk] = jnp.where(row < DA_QK_DIM, qt, zero)
        qq_sc[:, tq + c * tk:tq + (c + 1) * tk] = jnp.where(row >= DA_QK_DIM, qt, zero)

    l_sc[...] = jnp.zeros(l_sc.shape, F32)
    acc_sc[...] = jnp.zeros(acc_sc.shape, F32)

    strips = [slice(c * STRIP, (c + 1) * STRIP) for c in range(2 * tq // STRIP)]

    def key_chunk(t):
        return k_ref[pl.ds(pl.multiple_of(t * tk, tk), tk), :]

    def causal(diag, sl):
        q_lo = sl.start % tq
        if q_lo >= (diag + 1) * tk - 1:
            return None
        kpos = diag * tk + lax.broadcasted_iota(jnp.int32, (tk, STRIP), 0)
        qpos = q_lo + lax.broadcasted_iota(jnp.int32, (tk, STRIP), 1)
        return kpos <= qpos

    def visible(diag, sl):
        return (sl.start % tq) + STRIP - 1 >= diag * tk

    def probs_stages(t, par, diag=None):
        kj = key_chunk(t)
        for sl in strips:
            if diag is not None and not visible(diag, sl):
                continue
            p = jnp.exp2(jnp.dot(kj, qq_sc[:, sl], preferred_element_type=F32))
            mask = None if diag is None else causal(diag, sl)
            if mask is not None:
                p = jnp.where(mask, p, 0.0)
            l_sc[:, sl] += jnp.sum(p, axis=0, keepdims=True)
            p_sc[par][:, sl] = p.astype(BF16)
            yield

    def values_stages(t, par, diag=None):
        vt = vt_ref[t]
        for sl in strips:
            if diag is not None and not visible(diag, sl):
                continue
            acc_sc[:, sl] += jnp.dot(vt, p_sc[par][:, sl], preferred_element_type=F32)
            yield

    def probs(*args, **kwargs):
        _interleave(probs_stages(*args, **kwargs))

    def values_plain(*args, **kwargs):
        _interleave(values_stages(*args, **kwargs))

    def chain(*gens):
        for g in gens:
            yield from g

    def finalize_stages():
        lv = lam_ref[...]
        lam = (jnp.exp(jnp.sum(lv[0:1] * lv[1:2], axis=-1, keepdims=True))
               - jnp.exp(jnp.sum(lv[2:3] * lv[3:4], axis=-1, keepdims=True)) + LAMBDA_INIT)
        for c in range(tq // STRIP):
            m1 = slice(c * STRIP, (c + 1) * STRIP)
            m2 = slice(tq + c * STRIP, tq + (c + 1) * STRIP)
            o = acc_sc[:, m1] / l_sc[:, m1] - lam * (acc_sc[:, m2] / l_sc[:, m2])
            ms = jnp.mean(o * o, axis=0, keepdims=True)
            y = o * lax.rsqrt(ms + NORM_EPS) * g_ref[...] * (1.0 - LAMBDA_INIT)
            o_ref[m1, :] = y.T.astype(o_ref.dtype)
            yield

    def bounded_path():
        @pl.when(qi > 0)
        def _():
            probs(0, 0)

            def pair(t):
                probs(t + 1, 1)
                values_plain(t, 0)
                probs(t + 2, 0)
                values_plain(t + 1, 1)

            def two_pairs(i, carry):
                pair(4 * i)
                pair(4 * i + 2)
                return carry

            n_pairs = qi - 1
            lax.fori_loop(0, lax.shift_right_logical(n_pairs, 1), two_pairs, 0)

            @pl.when((n_pairs & 1) == 1)
            def _():
                pair(2 * n_pairs - 2)

            t = 2 * qi - 2
            probs(t + 1, 1)
            values_plain(t, 0)
            values_plain(t + 1, 1)

        d0 = 2 * qi
        _interleave(chain(probs_stages(d0, 0, diag=0), probs_stages(d0 + 1, 1, diag=1),
                          values_stages(d0, 0, diag=0), values_stages(d0 + 1, 1, diag=1),
                          finalize_stages()),
                    mlstm_stages())

    def scores(t, par):
        kj = key_chunk(t)
        for sl in strips:
            s = jnp.dot(kj, qq_sc[:, sl], preferred_element_type=F32)
            s_sc[par][:, sl] = s
            c_sc[par][:, sl] = jnp.max(s, axis=0, keepdims=True)

    def softmax(par, diag):
        for sl in strips:
            s = s_sc[par][:, sl]
            mask = None if diag is None else causal(diag, sl)
            if mask is None:
                cmax = c_sc[par][:, sl]
            else:
                s = jnp.where(mask, s, NEG_BIG)
                cmax = jnp.max(s, axis=0, keepdims=True)
            m_old = m_sc[:, sl]
            m_new = jnp.maximum(m_old, cmax)
            p = jnp.exp2(s - m_new)
            alpha = jnp.exp2(m_old - m_new)
            l_sc[:, sl] = alpha * l_sc[:, sl] + jnp.sum(p, axis=0, keepdims=True)
            m_sc[:, sl] = m_new
            a_sc[par][:, sl] = alpha
            p_sc[par][:, sl] = p.astype(BF16)

    def values(t, par):
        vt = vt_ref[t]
        for sl in strips:
            acc_sc[:, sl] = a_sc[par][:, sl] * acc_sc[:, sl] + jnp.dot(
                vt, p_sc[par][:, sl], preferred_element_type=F32)

    def online_path():
        m_sc[...] = jnp.full(m_sc.shape, NEG_BIG, F32)
        p1_sc[...] = jnp.zeros(p1_sc.shape, BF16)
        a1_sc[...] = jnp.ones(a1_sc.shape, F32)
        scores(0, 0)
        _interleave(mlstm_stages())

        def pair(i, carry):
            t = 2 * i
            scores(t + 1, 1)
            softmax(0, None)
            values(jnp.maximum(t - 1, 0), 1)
            scores(t + 2, 0)
            softmax(1, None)
            values(t, 0)
            return carry

        lax.fori_loop(0, qi, pair, 0)
        t = 2 * qi
        scores(t + 1, 1)
        softmax(0, 0)
        values(jnp.maximum(t - 1, 0), 1)
        softmax(1, 1)
        values(t, 0)
        values(t + 1, 1)
        _interleave(finalize_stages())

    pl.when(bounded_ref[0] == 1)(bounded_path)
    pl.when(bounded_ref[0] != 1)(online_path)


def _attn_mlstm(bounded, lam4, qt, kn, vt, g_col, proj3, gates3, gates_t, conv_w, conv_b, bias_row,
                bias_col, out_g, tq, tk):
    B, S, _ = kn.shape
    nk, nq = S // tk, S // tq
    L = tq // DA_HEADS
    assert tq == 2 * tk, "a query block spans exactly two key chunks"
    assert L % 8 == 0 and L * DA_HEADS * nq == S
    full = lambda shape: pl.BlockSpec(shape, lambda b, h, i, f: (0,) * len(shape))
    chunk = lambda width, col: pl.BlockSpec((None, L, width), lambda b, h, i, f: (b, h * nq + i, col))
    grid_spec = pltpu.PrefetchScalarGridSpec(
        num_scalar_prefetch=1,
        grid=(B, DA_HEADS, nq),
        in_specs=[
            full((4, DA_QK_DIM)),
            pl.BlockSpec((None, None, tq // tk, DA_HEAD_DIM, tk), lambda b, h, i, f: (b, h, i, 0, 0)),
            pl.BlockSpec((None, S, DA_HEAD_DIM), lambda b, h, i, f: (b, 0, h)),
            pl.BlockSpec((None, None, nk, DA_HEAD_DIM, tk), lambda b, h, i, f: (b, h, 0, 0, 0)),
            full((DA_HEAD_DIM, 1)),
            chunk(ML_QK_WIDTH, COL_ML_Q),
            chunk(ML_QK_WIDTH, COL_ML_K),
            chunk(ML_WIDTH, COL_ML_V),
            chunk(ML_WIDTH, COL_ML_O),
            chunk(LANES, 0),
            pl.BlockSpec((None, N_GATES, L), lambda b, h, i, f: (b, 0, h * nq + i)),
            full((CONV_WIDTH, 2 * ML_QK_WIDTH)),
            full((1, 2 * ML_QK_WIDTH)),
            full((1, LANES)),
            full((N_GATES, 1)),
            full((ML_HEADS, ML_V_DIM)),
        ],
        out_specs=[
            pl.BlockSpec((None, tq, DA_HEAD_DIM), lambda b, h, i, f: (b, i, h)),
            chunk(ML_WIDTH, 0),
        ],
        scratch_shapes=[
            pltpu.VMEM((DA_HEAD_DIM, 2 * tq), BF16),
            pltpu.VMEM((tk, 2 * tq), F32),
            pltpu.VMEM((tk, 2 * tq), F32),
            pltpu.VMEM((tk, 2 * tq), BF16),
            pltpu.VMEM((tk, 2 * tq), BF16),
            pltpu.VMEM((1, 2 * tq), F32),
            pltpu.VMEM((1, 2 * tq), F32),
            pltpu.VMEM((1, 2 * tq), F32),
            pltpu.VMEM((1, 2 * tq), F32),
            pltpu.VMEM((1, 2 * tq), F32),
            pltpu.VMEM((1, 2 * tq), F32),
            pltpu.VMEM((DA_HEAD_DIM, 2 * tq), F32),
            pltpu.VMEM((ML_HEADS, ML_QK_DIM, ML_V_DIM), F32),
            pltpu.VMEM((ML_HEADS, 1, ML_QK_DIM), F32),
            pltpu.VMEM((ML_HEADS, 1, 1), F32),
            pltpu.VMEM((L + 8, 2 * ML_QK_WIDTH), F32),
        ],
    )
    return pl.pallas_call(
        functools.partial(_attn_body, tq=tq, tk=tk),
        grid_spec=grid_spec,
        out_shape=[jax.ShapeDtypeStruct((B, S, DA_WIDTH), BF16),
                   jax.ShapeDtypeStruct((B, S, ML_WIDTH), BF16)],
        compiler_params=_cparams(("parallel", "arbitrary", "arbitrary")),
        name="diff_attn_mlstm",
    )(bounded, lam4, qt, kn, vt, g_col, proj3, proj3, proj3, proj3, gates3, gates_t,
      conv_w, conv_b, bias_row, bias_col, out_g)


def _outproj_body(x_ref, d_ref, m_ref, wd_ref, wm_ref, o_ref):
    o_ref[...] = (x_ref[...]
                  + jnp.dot(d_ref[...], wd_ref[...], preferred_element_type=F32)
                  + jnp.dot(m_ref[...], wm_ref[...], preferred_element_type=F32))


def _outproj(x2, d2, m2, w_out, bm):
    T = x2.shape[0]
    return pl.pallas_call(
        _outproj_body,
        grid=(T // bm,),
        in_specs=[
            pl.BlockSpec((bm, D_MODEL), lambda m: (m, 0)),
            pl.BlockSpec((bm, DA_WIDTH), lambda m: (m, 0)),
            pl.BlockSpec((bm, ML_WIDTH), lambda m: (m, 0)),
            pl.BlockSpec((DA_WIDTH, D_MODEL), lambda m: (0, 0)),
            pl.BlockSpec((ML_WIDTH, D_MODEL), lambda m: (1, 0)),
        ],
        out_specs=pl.BlockSpec((bm, D_MODEL), lambda m: (m, 0)),
        out_shape=jax.ShapeDtypeStruct((T, D_MODEL), F32),
        compiler_params=_cparams(("parallel",)),
        name="outproj",
    )(x2, d2, m2, w_out, w_out)


def _mlp_body(x_ref, g_ref, wu_ref, wd_ref, o_ref, h_ref):
    @pl.when(pl.program_id(1) == 0)
    def _():
        x = x_ref[...]
        ms = jnp.mean(x * x, axis=-1, keepdims=True)
        h_ref[...] = (x * lax.rsqrt(ms + NORM_EPS) * g_ref[...]).astype(BF16)
        o_ref[...] = x

    u = jnp.dot(h_ref[...], wu_ref[...], preferred_element_type=F32)
    a = jnp.square(jnp.maximum(u, 0.0)).astype(BF16)
    o_ref[...] += jnp.dot(a, wd_ref[...], preferred_element_type=F32)


def _mlp(x1, g, w_up, w_down, bm, tf):
    T = x1.shape[0]
    return pl.pallas_call(
        _mlp_body,
        grid=(T // bm, D_FF // tf),
        in_specs=[
            pl.BlockSpec((bm, D_MODEL), lambda m, f: (m, 0)),
            pl.BlockSpec((1, D_MODEL), lambda m, f: (0, 0)),
            pl.BlockSpec((D_MODEL, tf), lambda m, f: (0, f)),
            pl.BlockSpec((tf, D_MODEL), lambda m, f: (f, 0)),
        ],
        out_specs=pl.BlockSpec((bm, D_MODEL), lambda m, f: (m, 0)),
        out_shape=jax.ShapeDtypeStruct((T, D_MODEL), F32),
        scratch_shapes=[pltpu.VMEM((bm, D_MODEL), BF16)],
        compiler_params=_cparams(("parallel", "arbitrary")),
        name="mlp",
    )(x1, g, w_up, w_down)


def _tiles(B, S):
    T = B * S
    return dict(
        bm_in=min(1024, T), bn_in=1024,
        tp=min(512, S // 2),
        tq=min(1024, S),
        bm_out=min(512, T),
        bm_mlp=min(512, T), tf=1024,
    )


def kernel(x, norm1_g, w_in, ml_conv_w, ml_conv_b, ml_b_i, ml_b_f, ml_out_g, da_q_norm_g, da_k_norm_g, da_lambda_q1, da_lambda_k1, da_lambda_q2, da_lambda_k2, da_out_g, w_out, norm2_g, w_up, w_down):
    B, S, D = x.shape
    assert D == D_MODEL and norm1_g.shape[0] == 1, "single-layer kernel"
    t = _tiles(B, S)
    T = B * S
    x2 = x.reshape(T, D)

    w_in0 = w_in[0]
    w_main = jnp.transpose(w_in0[:, :D_MAIN].reshape(D_MODEL, D_MAIN // t["bn_in"], t["bn_in"]),
                           (1, 0, 2)).astype(BF16)
    w_gate = jnp.pad(w_in0[:, D_MAIN:], ((0, 0), (0, LANES - N_GATES))).astype(BF16)
    reps = STRIP // DA_QK_DIM
    gq2 = jnp.tile(da_q_norm_g[0], reps).reshape(1, STRIP) * (DA_QK_DIM ** -0.5 * math.log2(math.e))
    gk2 = jnp.tile(da_k_norm_g[0], reps).reshape(1, STRIP)
    score_bound = DA_QK_DIM * jnp.max(jnp.abs(gq2)) * jnp.max(jnp.abs(gk2))
    bounded = (score_bound <= SCORE_BOUND).astype(jnp.int32).reshape(1)
    lam4 = jnp.stack([da_lambda_q1[0], da_lambda_k1[0], da_lambda_q2[0], da_lambda_k2[0]])
    bias8 = jnp.concatenate([ml_b_i[0], ml_b_f[0]])
    bias_row = jnp.pad(bias8, (0, LANES - N_GATES)).reshape(1, LANES)
    bias_col = bias8.reshape(N_GATES, 1)

    proj, gates = _inproj(x2, norm1_g, w_main, w_gate, t["bm_in"], t["bn_in"])
    proj3 = proj.reshape(B, S, D_MAIN)
    gates3 = gates.reshape(B, S, LANES)
    gates_t = jnp.transpose(gates3[:, :, :N_GATES], (0, 2, 1))

    qt, kn, vt = _prep(proj3, gq2, gk2, t["tp"])
    d_out, m_out = _attn_mlstm(bounded, lam4, qt, kn, vt, da_out_g[0].reshape(DA_HEAD_DIM, 1),
                               proj3, gates3, gates_t, ml_conv_w[0], ml_conv_b, bias_row, bias_col,
                               ml_out_g[0], t["tq"], t["tp"])

    x1 = _outproj(x2, d_out.reshape(T, DA_WIDTH), m_out.reshape(T, ML_WIDTH),
                  w_out[0].astype(BF16), t["bm_out"])
    y = _mlp(x1, norm2_g, w_up[0].astype(BF16), w_down[0].astype(BF16), t["bm_mlp"], t["tf"])
    return y.reshape(B, S, D)
```

```python
import functools
import math

import jax
import jax.numpy as jnp
from jax import lax
from jax.experimental import pallas as pl
from jax.experimental.pallas import tpu as pltpu

F32 = jnp.float32
BF16 = jnp.bfloat16

D_MODEL = 2048
DA_HEADS = 8
DA_HEAD_DIM = 128
DA_QK_DIM = 64
DA_WIDTH = DA_HEADS * DA_HEAD_DIM
ML_HEADS = 4
ML_V_DIM = 256
ML_QK_DIM = 128
ML_WIDTH = ML_HEADS * ML_V_DIM
ML_QK_WIDTH = ML_HEADS * ML_QK_DIM
CONV_WIDTH = 4
D_FF = 4 * D_MODEL
D_MAIN = 3 * DA_WIDTH + 2 * ML_QK_WIDTH + 2 * ML_WIDTH
N_GATES = 2 * ML_HEADS
NORM_EPS = 1e-6
LAMBDA_INIT = 0.8 - 0.6 * math.exp(-0.3 * 0)
LANES = 128
STRIP = 256
NEG_BIG = -1e30
SCORE_BOUND = 60.0
VMEM_LIMIT = 56 * 1024 * 1024

COL_DA_Q, COL_DA_K, COL_DA_V = 0, 1, 2
COL_ML_Q, COL_ML_K = 6, 7
COL_ML_V, COL_ML_O = 4, 5


def _cparams(sem):
    return pltpu.CompilerParams(dimension_semantics=sem, vmem_limit_bytes=VMEM_LIMIT)


def _inproj_body(x_ref, g_ref, w_ref, wg_ref, o_ref, og_ref, h_ref):
    @pl.when(pl.program_id(1) == 0)
    def _():
        x = x_ref[...]
        ms = jnp.mean(x * x, axis=-1, keepdims=True)
        hb = (x * lax.rsqrt(ms + NORM_EPS) * g_ref[...]).astype(BF16)
        h_ref[...] = hb
        og_ref[...] = jnp.dot(hb, wg_ref[...], preferred_element_type=F32)

    o_ref[...] = jnp.dot(h_ref[...], w_ref[...], preferred_element_type=F32).astype(o_ref.dtype)


def _inproj(x2, g, w_main, w_gate, bm, bn):
    T = x2.shape[0]
    return pl.pallas_call(
        _inproj_body,
        grid=(T // bm, D_MAIN // bn),
        in_specs=[
            pl.BlockSpec((bm, D_MODEL), lambda m, n: (m, 0)),
            pl.BlockSpec((1, D_MODEL), lambda m, n: (0, 0)),
            pl.BlockSpec((D_MODEL, bn), lambda m, n: (0, n)),
            pl.BlockSpec((D_MODEL, LANES), lambda m, n: (0, 0)),
        ],
        out_specs=[
            pl.BlockSpec((bm, bn), lambda m, n: (m, n)),
            pl.BlockSpec((bm, LANES), lambda m, n: (m, 0)),
        ],
        out_shape=[
            jax.ShapeDtypeStruct((T, D_MAIN), BF16),
            jax.ShapeDtypeStruct((T, LANES), F32),
        ],
        scratch_shapes=[pltpu.VMEM((bm, D_MODEL), BF16)],
        compiler_params=_cparams(("parallel", "arbitrary")),
        name="inproj",
    )(x2, g, w_main, w_gate)


def _prep_body(q_ref, k_ref, v_ref, gq_ref, gk_ref, qt_ref, kn_ref, vt_ref):
    r = lax.broadcasted_iota(jnp.int32, (STRIP, STRIP), 0) // DA_QK_DIM
    c = lax.broadcasted_iota(jnp.int32, (STRIP, STRIP), 1) // DA_QK_DIM
    group = jnp.where(r == c, 1.0 / DA_QK_DIM, 0.0).astype(BF16)

    def norm(x, g):
        ms = jnp.dot((x * x).astype(BF16), group, preferred_element_type=F32)
        return x * lax.rsqrt(ms + NORM_EPS) * g

    for hp in range(DA_WIDTH // STRIP):
        sl = slice(hp * STRIP, (hp + 1) * STRIP)
        qn = norm(q_ref[:, sl].astype(F32), gq_ref[...])
        kn_ref[:, sl] = norm(k_ref[:, sl].astype(F32), gk_ref[...]).astype(BF16)
        v = v_ref[:, sl].astype(F32)
        for j in range(STRIP // DA_HEAD_DIM):
            h = hp * (STRIP // DA_HEAD_DIM) + j
            hs = slice(j * DA_HEAD_DIM, (j + 1) * DA_HEAD_DIM)
            qt_ref[h] = qn[:, hs].T.astype(BF16)
            vt_ref[h] = v[:, hs].T.astype(BF16)


def _prep(proj3, gq2, gk2, tp):
    B, S, _ = proj3.shape
    nk = S // tp
    blk = lambda col: pl.BlockSpec((None, tp, DA_WIDTH), lambda b, i, col=col: (b, i, col))
    vec = pl.BlockSpec((1, STRIP), lambda b, i: (0, 0))
    transposed = pl.BlockSpec((None, DA_HEADS, None, DA_HEAD_DIM, tp), lambda b, i: (b, 0, i, 0, 0))
    return pl.pallas_call(
        _prep_body,
        grid=(B, nk),
        in_specs=[blk(COL_DA_Q), blk(COL_DA_K), blk(COL_DA_V), vec, vec],
        out_specs=[
            transposed,
            pl.BlockSpec((None, tp, DA_WIDTH), lambda b, i: (b, i, 0)),
            transposed,
        ],
        out_shape=[
            jax.ShapeDtypeStruct((B, DA_HEADS, nk, DA_HEAD_DIM, tp), BF16),
            jax.ShapeDtypeStruct((B, S, DA_WIDTH), BF16),
            jax.ShapeDtypeStruct((B, DA_HEADS, nk, DA_HEAD_DIM, tp), BF16),
        ],
        compiler_params=_cparams(("parallel", "parallel")),
        name="attn_prep",
    )(proj3, proj3, proj3, gq2, gk2)


def _log_sigmoid(x):
    return jnp.minimum(x, 0.0) - jnp.log1p(jnp.exp(-jnp.abs(x)))


def _split3(x):
    hi = x.astype(BF16)
    r1 = x - hi.astype(F32)
    mid = r1.astype(BF16)
    lo = (r1 - mid.astype(F32)).astype(BF16)
    return hi, mid, lo


def _mlstm_reset(c_sc, n_sc, m_sc, ext_sc):
    c_sc[...] = jnp.zeros(c_sc.shape, F32)
    n_sc[...] = jnp.zeros(n_sc.shape, F32)
    m_sc[...] = jnp.zeros(m_sc.shape, F32)
    ext_sc[0:8, :] = jnp.zeros((8, 2 * ML_QK_WIDTH), F32)


def _mlstm_stages(mq_ref, mk_ref, mv_ref, mo_ref, gc_ref, gr_ref, cw_ref, cb_ref, brow_ref, bcol_ref,
                  og_ref, out_ref, c_sc, n_sc, m_sc, ext_sc, *, L):
    qk = []
    for src_ref in (mq_ref, mk_ref):
        for c in range(ML_QK_WIDTH // STRIP):
            sl = slice(c * STRIP, (c + 1) * STRIP)
            dst = slice(len(qk) * STRIP, (len(qk) + 1) * STRIP)
            x = src_ref[:, sl].astype(F32)
            ext_sc[8:8 + L, dst] = x
            y = cb_ref[:, dst] + cw_ref[0:1, dst] * ext_sc[5:5 + L, dst]
            y = y + cw_ref[1:2, dst] * ext_sc[6:6 + L, dst]
            y = y + cw_ref[2:3, dst] * ext_sc[7:7 + L, dst]
            y = y + cw_ref[3:4, dst] * x
            ext_sc[0:8, dst] = x[L - 8:L, :]
            qk.append(y * jax.nn.sigmoid(y))
            yield
    per_strip = STRIP // ML_QK_DIM
    head_cols = lambda j: qk[j // per_strip][:, (j % per_strip) * ML_QK_DIM:(j % per_strip + 1) * ML_QK_DIM]

    gc = gc_ref[...] + brow_ref[...]
    gr = gr_ref[...] + bcol_ref[...]
    lf_c = _log_sigmoid(gc)
    lf_r = _log_sigmoid(gr)
    ti = lax.broadcasted_iota(jnp.int32, (L, L), 0)
    si = lax.broadcasted_iota(jnp.int32, (L, L), 1)
    causal = si <= ti
    tril = causal.astype(BF16)
    triu = (ti <= si).astype(BF16)
    b_cols = sum(jnp.dot(tril, part, preferred_element_type=F32) for part in _split3(lf_c))
    b_rows = sum(jnp.dot(part, triu, preferred_element_type=F32) for part in _split3(lf_r))
    yield

    heads = range(ML_HEADS)
    q = [head_cols(h) * (ML_QK_DIM ** -0.5) for h in heads]
    k = [head_cols(ML_HEADS + h) for h in heads]
    v = [mv_ref[:, h * ML_V_DIM:(h + 1) * ML_V_DIM] for h in heads]
    i_col = [gc[:, h:h + 1] for h in heads]
    i_row = [gr[h:h + 1, :] for h in heads]
    b_col = [b_cols[:, ML_HEADS + h:ML_HEADS + h + 1] for h in heads]
    b_row = [b_rows[ML_HEADS + h:ML_HEADS + h + 1, :] for h in heads]
    num, den, m_t = {}, {}, {}

    for h in heads:
        m_prev = m_sc[h]
        log_inter = b_col[h] + m_prev
        dmat = jnp.where(causal, b_col[h] - b_row[h] + i_row[h], NEG_BIG)
        m_t[h] = jnp.maximum(log_inter, jnp.max(dmat, axis=1, keepdims=True))
        inter_w = jnp.exp(log_inter - m_t[h])
        qb = q[h].astype(BF16)
        s_qk = lax.dot_general(qb, k[h].astype(BF16), (((1,), (1,)), ((), ())),
                               preferred_element_type=F32)
        p = jnp.exp(dmat - m_t[h]) * s_qk
        num[h] = (inter_w * jnp.dot(qb, c_sc[h].astype(BF16), preferred_element_type=F32)
                  + jnp.dot(p.astype(BF16), v[h], preferred_element_type=F32))
        den[h] = (inter_w * jnp.sum(q[h] * n_sc[h], axis=1, keepdims=True)
                  + jnp.sum(p, axis=1, keepdims=True))
        yield

    for h in heads:
        hh = num[h] / jnp.maximum(jnp.abs(den[h]), jnp.exp(-m_t[h]))
        ms = jnp.mean(hh * hh, axis=1, keepdims=True)
        yh = hh * lax.rsqrt(ms + NORM_EPS) * og_ref[h:h + 1, :]
        gate = jax.nn.sigmoid(mo_ref[:, h * ML_V_DIM:(h + 1) * ML_V_DIM].astype(F32))
        out_ref[:, h * ML_V_DIM:(h + 1) * ML_V_DIM] = (yh * gate).astype(out_ref.dtype)
        yield

    for h in heads:
        m_prev = m_sc[h]
        a = b_col[h][L - 1:L, :]
        g_col = a - b_col[h] + i_col[h]
        g_max = jnp.max(g_col, axis=0, keepdims=True)
        kw = k[h] * jnp.exp(g_col - g_max)
        c_loc = lax.dot_general(kw.astype(BF16), v[h], (((0,), (0,)), ((), ())),
                                preferred_element_type=F32)
        n_loc = jnp.sum(kw, axis=0, keepdims=True)
        m_new = jnp.maximum(a + m_prev, g_max)
        decay = jnp.exp(a + m_prev - m_new)
        scale = jnp.exp(g_max - m_new)
        c_sc[h] = decay * c_sc[h] + scale * c_loc
        n_sc[h] = decay * n_sc[h] + scale * n_loc
        m_sc[h] = m_new
        yield


def _interleave(*stage_generators):
    live = list(stage_generators)
    while live:
        live = [g for g in live if next(g, StopIteration) is not StopIteration]


N_ATTN_IN, N_MLSTM_IN, N_ATTN_SCRATCH = 5, 11, 12


def _attn_body(bounded_ref, *refs, tq, tk):
    lam_ref, q_ref, k_ref, vt_ref, g_ref = refs[:N_ATTN_IN]
    ml_in = refs[N_ATTN_IN:N_ATTN_IN + N_MLSTM_IN]
    o_ref, ml_out = refs[N_ATTN_IN + N_MLSTM_IN:N_ATTN_IN + N_MLSTM_IN + 2]
    scratch = refs[N_ATTN_IN + N_MLSTM_IN + 2:]
    (qq_sc, s0_sc, s1_sc, p0_sc, p1_sc, a0_sc, a1_sc, c0_sc, c1_sc, m_sc, l_sc,
     acc_sc) = scratch[:N_ATTN_SCRATCH]
    ml_state = scratch[N_ATTN_SCRATCH:]
    s_sc, p_sc, a_sc, c_sc = (s0_sc, s1_sc), (p0_sc, p1_sc), (a0_sc, a1_sc), (c0_sc, c1_sc)
    qi = pl.program_id(2)

    @pl.when((pl.program_id(1) == 0) & (qi == 0))
    def _():
        _mlstm_reset(*ml_state)

    def mlstm_stages():
        return _mlstm_stages(*ml_in, ml_out, *ml_state, L=tq // DA_HEADS)

    row = lax.broadcasted_iota(jnp.int32, (DA_HEAD_DIM, tk), 0)
    for c in range(tq // tk):
        qt = q_ref[c]
        zero = jnp.zeros_like(qt)
        qq_sc[:, c * tk:(c + 1) * tk] = jnp.where(row < DA_QK_DIM, qt, zero)
        qq_sc[:, tq + c * tk:tq + (c + 1) * tk] = jnp.where(row >= DA_QK_DIM, qt, zero)

    l_sc[...] = jnp.zeros(l_sc.shape, F32)
    acc_sc[...] = jnp.zeros(acc_sc.shape, F32)

    strips = [slice(c * STRIP, (c + 1) * STRIP) for c in range(2 * tq // STRIP)]

    def key_chunk(t):
        return k_ref[pl.ds(pl.multiple_of(t * tk, tk), tk), :]

    def causal(diag, sl):
        q_lo = sl.start % tq
        if q_lo >= (diag + 1) * tk - 1:
            return None
        kpos = diag * tk + lax.broadcasted_iota(jnp.int32, (tk, STRIP), 0)
        qpos = q_lo + lax.broadcasted_iota(jnp.int32, (tk, STRIP), 1)
        return kpos <= qpos

    def visible(diag, sl):
        return (sl.start % tq) + STRIP - 1 >= diag * tk

    def probs_stages(t, par, diag=None):
        kj = key_chunk(t)
        for sl in strips:
            if diag is not None and not visible(diag, sl):
                continue
            p = jnp.exp2(jnp.dot(kj, qq_sc[:, sl], preferred_element_type=F32))
            mask = None if diag is None else causal(diag, sl)
            if mask is not None:
                p = jnp.where(mask, p, 0.0)
            l_sc[:, sl] += jnp.sum(p, axis=0, keepdims=True)
            p_sc[par][:, sl] = p.astype(BF16)
            yield

    def values_stages(t, par, diag=None):
        vt = vt_ref[t]
        for sl in strips:
            if diag is not None and not visible(diag, sl):
                continue
            acc_sc[:, sl] += jnp.dot(vt, p_sc[par][:, sl], preferred_element_type=F32)
            yield

    def probs(*args, **kwargs):
        _interleave(probs_stages(*args, **kwargs))

    def values_plain(*args, **kwargs):
        _interleave(values_stages(*args, **kwargs))

    def chain(*gens):
        for g in gens:
            yield from g

    def finalize_stages():
        lv = lam_ref[...]
        lam = (jnp.exp(jnp.sum(lv[0:1] * lv[1:2], axis=-1, keepdims=True))
               - jnp.exp(jnp.sum(lv[2:3] * lv[3:4], axis=-1, keepdims=True)) + LAMBDA_INIT)
        for c in range(tq // STRIP):
            m1 = slice(c * STRIP, (c + 1) * STRIP)
            m2 = slice(tq + c * STRIP, tq + (c + 1) * STRIP)
            o = acc_sc[:, m1] / l_sc[:, m1] - lam * (acc_sc[:, m2] / l_sc[:, m2])
            ms = jnp.mean(o * o, axis=0, keepdims=True)
            y = o * lax.rsqrt(ms + NORM_EPS) * g_ref[...] * (1.0 - LAMBDA_INIT)
            o_ref[m1, :] = y.T.astype(o_ref.dtype)
            yield

    def bounded_path():
        def pair(t):
            probs(t, 0)
            probs(t + 1, 1)
            values_plain(t, 0)
            values_plain(t + 1, 1)

        def two_pairs(i, carry):
            pair(4 * i)
            pair(4 * i + 2)
            return carry

        lax.fori_loop(0, lax.shift_right_logical(qi, 1), two_pairs, 0)

        @pl.when((qi & 1) == 1)
        def _():
            pair(2 * qi - 2)

        d0 = 2 * qi
        _interleave(chain(probs_stages(d0, 0, diag=0), probs_stages(d0 + 1, 1, diag=1),
                          values_stages(d0, 0, diag=0), values_stages(d0 + 1, 1, diag=1),
                          finalize_stages()),
                    mlstm_stages())

    def scores(t, par):
        kj = key_chunk(t)
        for sl in strips:
            s = jnp.dot(kj, qq_sc[:, sl], preferred_element_type=F32)
            s_sc[par][:, sl] = s
            c_sc[par][:, sl] = jnp.max(s, axis=0, keepdims=True)

    def softmax(par, diag):
        for sl in strips:
            s = s_sc[par][:, sl]
            mask = None if diag is None else causal(diag, sl)
            if mask is None:
                cmax = c_sc[par][:, sl]
            else:
                s = jnp.where(mask, s, NEG_BIG)
                cmax = jnp.max(s, axis=0, keepdims=True)
            m_old = m_sc[:, sl]
            m_new = jnp.maximum(m_old, cmax)
            p = jnp.exp2(s - m_new)
            alpha = jnp.exp2(m_old - m_new)
            l_sc[:, sl] = alpha * l_sc[:, sl] + jnp.sum(p, axis=0, keepdims=True)
            m_sc[:, sl] = m_new
            a_sc[par][:, sl] = alpha
            p_sc[par][:, sl] = p.astype(BF16)

    def values(t, par):
        vt = vt_ref[t]
        for sl in strips:
            acc_sc[:, sl] = a_sc[par][:, sl] * acc_sc[:, sl] + jnp.dot(
                vt, p_sc[par][:, sl], preferred_element_type=F32)

    def online_path():
        m_sc[...] = jnp.full(m_sc.shape, NEG_BIG, F32)
        p1_sc[...] = jnp.zeros(p1_sc.shape, BF16)
        a1_sc[...] = jnp.ones(a1_sc.shape, F32)
        scores(0, 0)
        _interleave(mlstm_stages())

        def pair(i, carry):
            t = 2 * i
            scores(t + 1, 1)
            softmax(0, None)
            values(jnp.maximum(t - 1, 0), 1)
            scores(t + 2, 0)
            softmax(1, None)
            values(t, 0)
            return carry

        lax.fori_loop(0, qi, pair, 0)
        t = 2 * qi
        scores(t + 1, 1)
        softmax(0, 0)
        values(jnp.maximum(t - 1, 0), 1)
        softmax(1, 1)
        values(t, 0)
        values(t + 1, 1)
        _interleave(finalize_stages())

    pl.when(bounded_ref[0] == 1)(bounded_path)
    pl.when(bounded_ref[0] != 1)(online_path)


def _attn_mlstm(bounded, lam4, qt, kn, vt, g_col, proj3, gates3, gates_t, conv_w, conv_b, bias_row,
                bias_col, out_g, tq, tk):
    B, S, _ = kn.shape
    nk, nq = S // tk, S // tq
    L = tq // DA_HEADS
    assert tq == 2 * tk, "a query block spans exactly two key chunks"
    assert L % 8 == 0 and L * DA_HEADS * nq == S
    full = lambda shape: pl.BlockSpec(shape, lambda b, h, i, f: (0,) * len(shape))
    chunk = lambda width, col: pl.BlockSpec((None, L, width), lambda b, h, i, f: (b, h * nq + i, col))
    grid_spec = pltpu.PrefetchScalarGridSpec(
        num_scalar_prefetch=1,
        grid=(B, DA_HEADS, nq),
        in_specs=[
            full((4, DA_QK_DIM)),
            pl.BlockSpec((None, None, tq // tk, DA_HEAD_DIM, tk), lambda b, h, i, f: (b, h, i, 0, 0)),
            pl.BlockSpec((None, S, DA_HEAD_DIM), lambda b, h, i, f: (b, 0, h)),
            pl.BlockSpec((None, None, nk, DA_HEAD_DIM, tk), lambda b, h, i, f: (b, h, 0, 0, 0)),
            full((DA_HEAD_DIM, 1)),
            chunk(ML_QK_WIDTH, COL_ML_Q),
            chunk(ML_QK_WIDTH, COL_ML_K),
            chunk(ML_WIDTH, COL_ML_V),
            chunk(ML_WIDTH, COL_ML_O),
            chunk(LANES, 0),
            pl.BlockSpec((None, N_GATES, L), lambda b, h, i, f: (b, 0, h * nq + i)),
            full((CONV_WIDTH, 2 * ML_QK_WIDTH)),
            full((1, 2 * ML_QK_WIDTH)),
            full((1, LANES)),
            full((N_GATES, 1)),
            full((ML_HEADS, ML_V_DIM)),
        ],
        out_specs=[
            pl.BlockSpec((None, tq, DA_HEAD_DIM), lambda b, h, i, f: (b, i, h)),
            chunk(ML_WIDTH, 0),
        ],
        scratch_shapes=[
            pltpu.VMEM((DA_HEAD_DIM, 2 * tq), BF16),
            pltpu.VMEM((tk, 2 * tq), F32),
            pltpu.VMEM((tk, 2 * tq), F32),
            pltpu.VMEM((tk, 2 * tq), BF16),
            pltpu.VMEM((tk, 2 * tq), BF16),
            pltpu.VMEM((1, 2 * tq), F32),
            pltpu.VMEM((1, 2 * tq), F32),
            pltpu.VMEM((1, 2 * tq), F32),
            pltpu.VMEM((1, 2 * tq), F32),
            pltpu.VMEM((1, 2 * tq), F32),
            pltpu.VMEM((1, 2 * tq), F32),
            pltpu.VMEM((DA_HEAD_DIM, 2 * tq), F32),
            pltpu.VMEM((ML_HEADS, ML_QK_DIM, ML_V_DIM), F32),
            pltpu.VMEM((ML_HEADS, 1, ML_QK_DIM), F32),
            pltpu.VMEM((ML_HEADS, 1, 1), F32),
            pltpu.VMEM((L + 8, 2 * ML_QK_WIDTH), F32),
        ],
    )
    return pl.pallas_call(
        functools.partial(_attn_body, tq=tq, tk=tk),
        grid_spec=grid_spec,
        out_shape=[jax.ShapeDtypeStruct((B, S, DA_WIDTH), BF16),
                   jax.ShapeDtypeStruct((B, S, ML_WIDTH), BF16)],
        compiler_params=_cparams(("parallel", "arbitrary", "arbitrary")),
        name="diff_attn_mlstm",
    )(bounded, lam4, qt, kn, vt, g_col, proj3, proj3, proj3, proj3, gates3, gates_t,
      conv_w, conv_b, bias_row, bias_col, out_g)


def _outproj_body(x_ref, d_ref, m_ref, wd_ref, wm_ref, o_ref):
    o_ref[...] = (x_ref[...]
                  + jnp.dot(d_ref[...], wd_ref[...], preferred_element_type=F32)
                  + jnp.dot(m_ref[...], wm_ref[...], preferred_element_type=F32))


def _outproj(x2, d2, m2, w_out, bm):
    T = x2.shape[0]
    return pl.pallas_call(
        _outproj_body,
        grid=(T // bm,),
        in_specs=[
            pl.BlockSpec((bm, D_MODEL), lambda m: (m, 0)),
            pl.BlockSpec((bm, DA_WIDTH), lambda m: (m, 0)),
            pl.BlockSpec((bm, ML_WIDTH), lambda m: (m, 0)),
            pl.BlockSpec((DA_WIDTH, D_MODEL), lambda m: (0, 0)),
            pl.BlockSpec((ML_WIDTH, D_MODEL), lambda m: (1, 0)),
        ],
        out_specs=pl.BlockSpec((bm, D_MODEL), lambda m: (m, 0)),
        out_shape=jax.ShapeDtypeStruct((T, D_MODEL), F32),
        compiler_params=_cparams(("parallel",)),
        name="outproj",
    )(x2, d2, m2, w_out, w_out)


def _mlp_body(x_ref, g_ref, wu_ref, wd_ref, o_ref, h_ref):
    @pl.when(pl.program_id(1) == 0)
    def _():
        x = x_ref[...]
        ms = jnp.mean(x * x, axis=-1, keepdims=True)
        h_ref[...] = (x * lax.rsqrt(ms + NORM_EPS) * g_ref[...]).astype(BF16)
        o_ref[...] = x

    u = jnp.dot(h_ref[...], wu_ref[...], preferred_element_type=F32)
    a = jnp.square(jnp.maximum(u, 0.0)).astype(BF16)
    o_ref[...] += jnp.dot(a, wd_ref[...], preferred_element_type=F32)


def _mlp(x1, g, w_up, w_down, bm, tf):
    T = x1.shape[0]
    return pl.pallas_call(
        _mlp_body,
        grid=(T // bm, D_FF // tf),
        in_specs=[
            pl.BlockSpec((bm, D_MODEL), lambda m, f: (m, 0)),
            pl.BlockSpec((1, D_MODEL), lambda m, f: (0, 0)),
            pl.BlockSpec((D_MODEL, tf), lambda m, f: (0, f)),
            pl.BlockSpec((tf, D_MODEL), lambda m, f: (f, 0)),
        ],
        out_specs=pl.BlockSpec((bm, D_MODEL), lambda m, f: (m, 0)),
        out_shape=jax.ShapeDtypeStruct((T, D_MODEL), F32),
        scratch_shapes=[pltpu.VMEM((bm, D_MODEL), BF16)],
        compiler_params=_cparams(("parallel", "arbitrary")),
        name="mlp",
    )(x1, g, w_up, w_down)


def _tiles(B, S):
    T = B * S
    return dict(
        bm_in=min(1024, T), bn_in=1024,
        tp=min(512, S // 2),
        tq=min(1024, S),
        bm_out=min(512, T),
        bm_mlp=min(512, T), tf=1024,
    )


def kernel(x, norm1_g, w_in, ml_conv_w, ml_conv_b, ml_b_i, ml_b_f, ml_out_g, da_q_norm_g, da_k_norm_g, da_lambda_q1, da_lambda_k1, da_lambda_q2, da_lambda_k2, da_out_g, w_out, norm2_g, w_up, w_down):
    B, S, D = x.shape
    assert D == D_MODEL and norm1_g.shape[0] == 1, "single-layer kernel"
    t = _tiles(B, S)
    T = B * S
    x2 = x.reshape(T, D)

    w_in0 = w_in[0]
    w_main = w_in0.astype(BF16)
    w_gate = jnp.pad(w_in0[:, D_MAIN:], ((0, 0), (0, LANES - N_GATES))).astype(BF16)
    reps = STRIP // DA_QK_DIM
    gq2 = jnp.tile(da_q_norm_g[0], reps).reshape(1, STRIP) * (DA_QK_DIM ** -0.5 * math.log2(math.e))
    gk2 = jnp.tile(da_k_norm_g[0], reps).reshape(1, STRIP)
    score_bound = DA_QK_DIM * jnp.max(jnp.abs(gq2)) * jnp.max(jnp.abs(gk2))
    bounded = (score_bound <= SCORE_BOUND).astype(jnp.int32).reshape(1)
    lam4 = jnp.stack([da_lambda_q1[0], da_lambda_k1[0], da_lambda_q2[0], da_lambda_k2[0]])
    bias8 = jnp.concatenate([ml_b_i[0], ml_b_f[0]])
    bias_row = jnp.pad(bias8, (0, LANES - N_GATES)).reshape(1, LANES)
    bias_col = bias8.reshape(N_GATES, 1)

    proj, gates = _inproj(x2, norm1_g, w_main, w_gate, t["bm_in"], t["bn_in"])
    proj3 = proj.reshape(B, S, D_MAIN)
    gates3 = gates.reshape(B, S, LANES)
    gates_t = jnp.transpose(gates3[:, :, :N_GATES], (0, 2, 1))

    qt, kn, vt = _prep(proj3, gq2, gk2, t["tp"])
    d_out, m_out = _attn_mlstm(bounded, lam4, qt, kn, vt, da_out_g[0].reshape(DA_HEAD_DIM, 1),
                               proj3, gates3, gates_t, ml_conv_w[0], ml_conv_b, bias_row, bias_col,
                               ml_out_g[0], t["tq"], t["tp"])

    x1 = _outproj(x2, d_out.reshape(T, DA_WIDTH), m_out.reshape(T, ML_WIDTH),
                  w_out[0].astype(BF16), t["bm_out"])
    y = _mlp(x1, norm2_g, w_up[0].astype(BF16), w_down[0].astype(BF16), t["bm_mlp"], t["tf"])
    return y.reshape(B, S, D)
```

```python
import functools
import math

import jax
import jax.numpy as jnp
from jax import lax
from jax.experimental import pallas as pl
from jax.experimental.pallas import tpu as pltpu

F32 = jnp.float32
BF16 = jnp.bfloat16

D_MODEL = 2048
DA_HEADS = 8
DA_HEAD_DIM = 128
DA_QK_DIM = 64
DA_WIDTH = DA_HEADS * DA_HEAD_DIM
ML_HEADS = 4
ML_V_DIM = 256
ML_QK_DIM = 128
ML_WIDTH = ML_HEADS * ML_V_DIM
ML_QK_WIDTH = ML_HEADS * ML_QK_DIM
CONV_WIDTH = 4
D_FF = 4 * D_MODEL
D_MAIN = 3 * DA_WIDTH + 2 * ML_QK_WIDTH + 2 * ML_WIDTH
N_GATES = 2 * ML_HEADS
NORM_EPS = 1e-6
LAMBDA_INIT = 0.8 - 0.6 * math.exp(-0.3 * 0)
LANES = 128
STRIP = 256
NEG_BIG = -1e30
SCORE_BOUND = 60.0
VMEM_LIMIT = 56 * 1024 * 1024

COL_DA_Q, COL_DA_K, COL_DA_V = 0, 1, 2
COL_ML_Q, COL_ML_K = 6, 7
COL_ML_V, COL_ML_O = 4, 5


def _cparams(sem):
    return pltpu.CompilerParams(dimension_semantics=sem, vmem_limit_bytes=VMEM_LIMIT)


def _inproj_body(x_ref, g_ref, w_ref, wg_ref, o_ref, og_ref, h_ref):
    @pl.when(pl.program_id(1) == 0)
    def _():
        x = x_ref[...]
        ms = jnp.mean(x * x, axis=-1, keepdims=True)
        hb = (x * lax.rsqrt(ms + NORM_EPS) * g_ref[...]).astype(BF16)
        h_ref[...] = hb
        og_ref[...] = jnp.dot(hb, wg_ref[...], preferred_element_type=F32)

    o_ref[...] = jnp.dot(h_ref[...], w_ref[...], preferred_element_type=F32).astype(o_ref.dtype)


def _inproj(x2, g, w_main, w_gate, bm, bn):
    T = x2.shape[0]
    return pl.pallas_call(
        _inproj_body,
        grid=(T // bm, D_MAIN // bn),
        in_specs=[
            pl.BlockSpec((bm, D_MODEL), lambda m, n: (m, 0)),
            pl.BlockSpec((1, D_MODEL), lambda m, n: (0, 0)),
            pl.BlockSpec((D_MODEL, bn), lambda m, n: (0, n)),
            pl.BlockSpec((D_MODEL, LANES), lambda m, n: (0, 0)),
        ],
        out_specs=[
            pl.BlockSpec((bm, bn), lambda m, n: (m, n)),
            pl.BlockSpec((bm, LANES), lambda m, n: (m, 0)),
        ],
        out_shape=[
            jax.ShapeDtypeStruct((T, D_MAIN), BF16),
            jax.ShapeDtypeStruct((T, LANES), F32),
        ],
        scratch_shapes=[pltpu.VMEM((bm, D_MODEL), BF16)],
        compiler_params=_cparams(("parallel", "arbitrary")),
        name="inproj",
    )(x2, g, w_main, w_gate)


def _prep_body(q_ref, k_ref, v_ref, gq_ref, gk_ref, qt_ref, kn_ref, vt_ref):
    r = lax.broadcasted_iota(jnp.int32, (STRIP, STRIP), 0) // DA_QK_DIM
    c = lax.broadcasted_iota(jnp.int32, (STRIP, STRIP), 1) // DA_QK_DIM
    group = jnp.where(r == c, 1.0 / DA_QK_DIM, 0.0).astype(BF16)

    def norm(x, g):
        ms = jnp.dot((x * x).astype(BF16), group, preferred_element_type=F32)
        return x * lax.rsqrt(ms + NORM_EPS) * g

    for hp in range(DA_WIDTH // STRIP):
        sl = slice(hp * STRIP, (hp + 1) * STRIP)
        qn = norm(q_ref[:, sl].astype(F32), gq_ref[...])
        kn_ref[:, sl] = norm(k_ref[:, sl].astype(F32), gk_ref[...]).astype(BF16)
        v = v_ref[:, sl].astype(F32)
        for j in range(STRIP // DA_HEAD_DIM):
            h = hp * (STRIP // DA_HEAD_DIM) + j
            hs = slice(j * DA_HEAD_DIM, (j + 1) * DA_HEAD_DIM)
            qt_ref[h] = qn[:, hs].T.astype(BF16)
            vt_ref[h] = v[:, hs].T.astype(BF16)


def _prep(proj3, gq2, gk2, tp):
    B, S, _ = proj3.shape
    nk = S // tp
    blk = lambda col: pl.BlockSpec((None, tp, DA_WIDTH), lambda b, i, col=col: (b, i, col))
    vec = pl.BlockSpec((1, STRIP), lambda b, i: (0, 0))
    transposed = pl.BlockSpec((None, DA_HEADS, None, DA_HEAD_DIM, tp), lambda b, i: (b, 0, i, 0, 0))
    return pl.pallas_call(
        _prep_body,
        grid=(B, nk),
        in_specs=[blk(COL_DA_Q), blk(COL_DA_K), blk(COL_DA_V), vec, vec],
        out_specs=[
            transposed,
            pl.BlockSpec((None, tp, DA_WIDTH), lambda b, i: (b, i, 0)),
            transposed,
        ],
        out_shape=[
            jax.ShapeDtypeStruct((B, DA_HEADS, nk, DA_HEAD_DIM, tp), BF16),
            jax.ShapeDtypeStruct((B, S, DA_WIDTH), BF16),
            jax.ShapeDtypeStruct((B, DA_HEADS, nk, DA_HEAD_DIM, tp), BF16),
        ],
        compiler_params=_cparams(("parallel", "parallel")),
        name="attn_prep",
    )(proj3, proj3, proj3, gq2, gk2)


def _log_sigmoid(x):
    return jnp.minimum(x, 0.0) - jnp.log1p(jnp.exp(-jnp.abs(x)))


def _split3(x):
    hi = x.astype(BF16)
    r1 = x - hi.astype(F32)
    mid = r1.astype(BF16)
    lo = (r1 - mid.astype(F32)).astype(BF16)
    return hi, mid, lo


def _mlstm_reset(c_sc, n_sc, m_sc, tail_sc, tri_sc, bias_sc, *, L):
    c_sc[...] = jnp.zeros(c_sc.shape, F32)
    n_sc[...] = jnp.zeros(n_sc.shape, F32)
    m_sc[...] = jnp.zeros(m_sc.shape, F32)
    tail_sc[...] = jnp.zeros(tail_sc.shape, F32)
    ti = lax.broadcasted_iota(jnp.int32, (L, L), 0)
    si = lax.broadcasted_iota(jnp.int32, (L, L), 1)
    tri_sc[0] = (si <= ti).astype(BF16)
    tri_sc[1] = (ti <= si).astype(BF16)
    for j in range(1, CONV_WIDTH):
        tri_sc[1 + j] = (si == ti - j).astype(BF16)
    bias_sc[...] = jnp.where(si <= ti, 0.0, NEG_BIG)


def _mlstm_stages(mq_ref, mk_ref, mv_ref, mo_ref, gc_ref, gr_ref, cw_ref, cb_ref, brow_ref, bcol_ref,
                  og_ref, out_ref, c_sc, n_sc, m_sc, tail_sc, tri_sc, bias_sc, *, L):
    gc = gc_ref[...] + brow_ref[...]
    gr = gr_ref[...] + bcol_ref[...]
    lf_c = _log_sigmoid(gc)
    lf_r = _log_sigmoid(gr)
    b_cols = sum(jnp.dot(tri_sc[0], part, preferred_element_type=F32) for part in _split3(lf_c))
    b_rows = sum(jnp.dot(part, tri_sc[1], preferred_element_type=F32) for part in _split3(lf_r))
    yield

    qk = []
    for src_ref in (mq_ref, mk_ref):
        for c in range(ML_QK_WIDTH // STRIP):
            sl = slice(c * STRIP, (c + 1) * STRIP)
            dst = slice(len(qk) * STRIP, (len(qk) + 1) * STRIP)
            xb = src_ref[:, sl]
            x = xb.astype(F32)
            y = cb_ref[:, dst] + cw_ref[CONV_WIDTH - 1:CONV_WIDTH, dst] * x
            fix = jnp.zeros((8, STRIP), F32)
            for j in range(1, CONV_WIDTH):
                w = cw_ref[CONV_WIDTH - 1 - j:CONV_WIDTH - j, dst]
                y = y + w * jnp.dot(tri_sc[1 + j], xb, preferred_element_type=F32)
                fix = fix + w * tail_sc[8 - j:16 - j, dst]
            y = jnp.concatenate([y[:8] + fix, y[8:]], axis=0)
            tail_sc[0:8, dst] = x[L - 8:L, :]
            half = 0.5 * y
            qk.append(half + half * jnp.tanh(half))
            yield
    per_strip = STRIP // ML_QK_DIM
    head_cols = lambda j: qk[j // per_strip][:, (j % per_strip) * ML_QK_DIM:(j % per_strip + 1) * ML_QK_DIM]

    heads = range(ML_HEADS)
    q = [head_cols(h) * (ML_QK_DIM ** -0.5) for h in heads]
    k = [head_cols(ML_HEADS + h) for h in heads]
    v = [mv_ref[:, h * ML_V_DIM:(h + 1) * ML_V_DIM] for h in heads]
    i_col = [gc[:, h:h + 1] for h in heads]
    i_row = [gr[h:h + 1, :] for h in heads]
    b_col = [b_cols[:, ML_HEADS + h:ML_HEADS + h + 1] for h in heads]
    b_row = [b_rows[ML_HEADS + h:ML_HEADS + h + 1, :] for h in heads]
    num, den, m_t = {}, {}, {}

    for h in heads:
        m_prev = m_sc[h]
        log_inter = b_col[h] + m_prev
        dmat = b_col[h] - b_row[h] + i_row[h] + bias_sc[...]
        m_t[h] = jnp.maximum(log_inter, jnp.max(dmat, axis=1, keepdims=True))
        inter_w = jnp.exp(log_inter - m_t[h])
        qb = q[h].astype(BF16)
        s_qk = lax.dot_general(qb, k[h].astype(BF16), (((1,), (1,)), ((), ())),
                               preferred_element_type=F32)
        p = jnp.exp(dmat - m_t[h]) * s_qk
        num[h] = (inter_w * jnp.dot(qb, c_sc[h].astype(BF16), preferred_element_type=F32)
                  + jnp.dot(p.astype(BF16), v[h], preferred_element_type=F32))
        den[h] = (inter_w * jnp.sum(q[h] * n_sc[h], axis=1, keepdims=True)
                  + jnp.sum(p, axis=1, keepdims=True))
        yield

    for h in heads:
        hh = num[h] * (1.0 / jnp.maximum(jnp.abs(den[h]), jnp.exp(-m_t[h])))
        ms = jnp.mean(hh * hh, axis=1, keepdims=True)
        yh = hh * lax.rsqrt(ms + NORM_EPS) * og_ref[h:h + 1, :]
        o_half = 0.5 * mo_ref[:, h * ML_V_DIM:(h + 1) * ML_V_DIM].astype(F32)
        gate = 0.5 + 0.5 * jnp.tanh(o_half)
        out_ref[:, h * ML_V_DIM:(h + 1) * ML_V_DIM] = (yh * gate).astype(out_ref.dtype)
        yield

    for h in heads:
        m_prev = m_sc[h]
        a = b_col[h][L - 1:L, :]
        g_col = a - b_col[h] + i_col[h]
        g_max = jnp.max(g_col, axis=0, keepdims=True)
        kw = k[h] * jnp.exp(g_col - g_max)
        c_loc = lax.dot_general(kw.astype(BF16), v[h], (((0,), (0,)), ((), ())),
                                preferred_element_type=F32)
        n_loc = jnp.sum(kw, axis=0, keepdims=True)
        m_new = jnp.maximum(a + m_prev, g_max)
        decay = jnp.exp(a + m_prev - m_new)
        scale = jnp.exp(g_max - m_new)
        c_sc[h] = decay * c_sc[h] + scale * c_loc
        n_sc[h] = decay * n_sc[h] + scale * n_loc
        m_sc[h] = m_new
        yield


def _interleave(*stage_generators):
    live = list(stage_generators)
    while live:
        live = [g for g in live if next(g, StopIteration) is not StopIteration]


N_ATTN_IN, N_MLSTM_IN, N_ATTN_SCRATCH = 5, 11, 12


def _attn_body(bounded_ref, *refs, tq, tk):
    lam_ref, q_ref, k_ref, vt_ref, g_ref = refs[:N_ATTN_IN]
    ml_in = refs[N_ATTN_IN:N_ATTN_IN + N_MLSTM_IN]
    o_ref, ml_out = refs[N_ATTN_IN + N_MLSTM_IN:N_ATTN_IN + N_MLSTM_IN + 2]
    scratch = refs[N_ATTN_IN + N_MLSTM_IN + 2:]
    (qq_sc, s0_sc, s1_sc, p0_sc, p1_sc, a0_sc, a1_sc, c0_sc, c1_sc, m_sc, l_sc,
     acc_sc) = scratch[:N_ATTN_SCRATCH]
    ml_state = scratch[N_ATTN_SCRATCH:]
    s_sc, p_sc, a_sc, c_sc = (s0_sc, s1_sc), (p0_sc, p1_sc), (a0_sc, a1_sc), (c0_sc, c1_sc)
    qi = pl.program_id(2)

    @pl.when((pl.program_id(1) == 0) & (qi == 0))
    def _():
        _mlstm_reset(*ml_state, L=tq // DA_HEADS)

    def mlstm_stages():
        return _mlstm_stages(*ml_in, ml_out, *ml_state, L=tq // DA_HEADS)

    row = lax.broadcasted_iota(jnp.int32, (DA_HEAD_DIM, tk), 0)
    for c in range(tq // tk):
        qt = q_ref[c]
        zero = jnp.zeros_like(qt)
        qq_sc[:, c * tk:(c + 1) * tk] = jnp.where(row < DA_QK_DIM, qt, zero)
        qq_sc[:, tq + c * tk:tq + (c + 1) * tk] = jnp.where(row >= DA_QK_DIM, qt, zero)

    l_sc[...] = jnp.zeros(l_sc.shape, F32)
    acc_sc[...] = jnp.zeros(acc_sc.shape, F32)

    strips = [slice(c * STRIP, (c + 1) * STRIP) for c in range(2 * tq // STRIP)]

    def key_chunk(t):
        return k_ref[pl.ds(pl.multiple_of(t * tk, tk), tk), :]

    def causal(diag, sl):
        q_lo = sl.start % tq
        if q_lo >= (diag + 1) * tk - 1:
            return None
        kpos = diag * tk + lax.broadcasted_iota(jnp.int32, (tk, STRIP), 0)
        qpos = q_lo + lax.broadcasted_iota(jnp.int32, (tk, STRIP), 1)
        return kpos <= qpos

    def visible(diag, sl):
        return (sl.start % tq) + STRIP - 1 >= diag * tk

    def probs_stages(t, par, diag=None):
        kj = key_chunk(t)
        for sl in strips:
            if diag is not None and not visible(diag, sl):
                continue
            p = jnp.exp2(jnp.dot(kj, qq_sc[:, sl], preferred_element_type=F32))
            mask = None if diag is None else causal(diag, sl)
            if mask is not None:
                p = jnp.where(mask, p, 0.0)
            l_sc[:, sl] += jnp.sum(p, axis=0, keepdims=True)
            p_sc[par][:, sl] = p.astype(BF16)
            yield

    def values_stages(t, par, diag=None):
        vt = vt_ref[t]
        for sl in strips:
            if diag is not None and not visible(diag, sl):
                continue
            acc_sc[:, sl] += jnp.dot(vt, p_sc[par][:, sl], preferred_element_type=F32)
            yield

    def probs(*args, **kwargs):
        _interleave(probs_stages(*args, **kwargs))

    def values_plain(*args, **kwargs):
        _interleave(values_stages(*args, **kwargs))

    def chain(*gens):
        for g in gens:
            yield from g

    def finalize_stages():
        lv = lam_ref[...]
        lam = (jnp.exp(jnp.sum(lv[0:1] * lv[1:2], axis=-1, keepdims=True))
               - jnp.exp(jnp.sum(lv[2:3] * lv[3:4], axis=-1, keepdims=True)) + LAMBDA_INIT)
        for c in range(tq // STRIP):
            m1 = slice(c * STRIP, (c + 1) * STRIP)
            m2 = slice(tq + c * STRIP, tq + (c + 1) * STRIP)
            o = acc_sc[:, m1] / l_sc[:, m1] - lam * (acc_sc[:, m2] / l_sc[:, m2])
            ms = jnp.mean(o * o, axis=0, keepdims=True)
            y = o * lax.rsqrt(ms + NORM_EPS) * g_ref[...] * (1.0 - LAMBDA_INIT)
            o_ref[m1, :] = y.T.astype(o_ref.dtype)
            yield

    def bounded_path():
        def pair(t):
            probs(t, 0)
            probs(t + 1, 1)
            values_plain(t, 0)
            values_plain(t + 1, 1)

        def two_pairs(i, carry):
            pair(4 * i)
            pair(4 * i + 2)
            return carry

        lax.fori_loop(0, lax.shift_right_logical(qi, 1), two_pairs, 0)

        @pl.when((qi & 1) == 1)
        def _():
            pair(2 * qi - 2)

        d0 = 2 * qi
        _interleave(chain(probs_stages(d0, 0, diag=0), probs_stages(d0 + 1, 1, diag=1),
                          values_stages(d0, 0, diag=0), values_stages(d0 + 1, 1, diag=1),
                          finalize_stages()),
                    mlstm_stages())

    def scores(t, par):
        kj = key_chunk(t)
        for sl in strips:
            s = jnp.dot(kj, qq_sc[:, sl], preferred_element_type=F32)
            s_sc[par][:, sl] = s
            c_sc[par][:, sl] = jnp.max(s, axis=0, keepdims=True)

    def softmax(par, diag):
        for sl in strips:
            s = s_sc[par][:, sl]
            mask = None if diag is None else causal(diag, sl)
            if mask is None:
                cmax = c_sc[par][:, sl]
            else:
                s = jnp.where(mask, s, NEG_BIG)
                cmax = jnp.max(s, axis=0, keepdims=True)
            m_old = m_sc[:, sl]
            m_new = jnp.maximum(m_old, cmax)
            p = jnp.exp2(s - m_new)
            alpha = jnp.exp2(m_old - m_new)
            l_sc[:, sl] = alpha * l_sc[:, sl] + jnp.sum(p, axis=0, keepdims=True)
            m_sc[:, sl] = m_new
            a_sc[par][:, sl] = alpha
            p_sc[par][:, sl] = p.astype(BF16)

    def values(t, par):
        vt = vt_ref[t]
        for sl in strips:
            acc_sc[:, sl] = a_sc[par][:, sl] * acc_sc[:, sl] + jnp.dot(
                vt, p_sc[par][:, sl], preferred_element_type=F32)

    def online_path():
        m_sc[...] = jnp.full(m_sc.shape, NEG_BIG, F32)
        p1_sc[...] = jnp.zeros(p1_sc.shape, BF16)
        a1_sc[...] = jnp.ones(a1_sc.shape, F32)
        scores(0, 0)
        _interleave(mlstm_stages())

        def pair(i, carry):
            t = 2 * i
            scores(t + 1, 1)
            softmax(0, None)
            values(jnp.maximum(t - 1, 0), 1)
            scores(t + 2, 0)
            softmax(1, None)
            values(t, 0)
            return carry

        lax.fori_loop(0, qi, pair, 0)
        t = 2 * qi
        scores(t + 1, 1)
        softmax(0, 0)
        values(jnp.maximum(t - 1, 0), 1)
        softmax(1, 1)
        values(t, 0)
        values(t + 1, 1)
        _interleave(finalize_stages())

    pl.when(bounded_ref[0] == 1)(bounded_path)
    pl.when(bounded_ref[0] != 1)(online_path)


def _attn_mlstm(bounded, lam4, qt, kn, vt, g_col, proj3, gates3, gates_t, conv_w, conv_b, bias_row,
                bias_col, out_g, tq, tk):
    B, S, _ = kn.shape
    nk, nq = S // tk, S // tq
    L = tq // DA_HEADS
    assert tq == 2 * tk, "a query block spans exactly two key chunks"
    assert L % 8 == 0 and L * DA_HEADS * nq == S
    full = lambda shape: pl.BlockSpec(shape, lambda b, h, i, f: (0,) * len(shape))
    chunk = lambda width, col: pl.BlockSpec((None, L, width), lambda b, h, i, f: (b, h * nq + i, col))
    grid_spec = pltpu.PrefetchScalarGridSpec(
        num_scalar_prefetch=1,
        grid=(B, DA_HEADS, nq),
        in_specs=[
            full((4, DA_QK_DIM)),
            pl.BlockSpec((None, None, tq // tk, DA_HEAD_DIM, tk), lambda b, h, i, f: (b, h, i, 0, 0)),
            pl.BlockSpec((None, S, DA_HEAD_DIM), lambda b, h, i, f: (b, 0, h)),
            pl.BlockSpec((None, None, nk, DA_HEAD_DIM, tk), lambda b, h, i, f: (b, h, 0, 0, 0)),
            full((DA_HEAD_DIM, 1)),
            chunk(ML_QK_WIDTH, COL_ML_Q),
            chunk(ML_QK_WIDTH, COL_ML_K),
            chunk(ML_WIDTH, COL_ML_V),
            chunk(ML_WIDTH, COL_ML_O),
            chunk(LANES, 0),
            pl.BlockSpec((None, N_GATES, L), lambda b, h, i, f: (b, 0, h * nq + i)),
            full((CONV_WIDTH, 2 * ML_QK_WIDTH)),
            full((1, 2 * ML_QK_WIDTH)),
            full((1, LANES)),
            full((N_GATES, 1)),
            full((ML_HEADS, ML_V_DIM)),
        ],
        out_specs=[
            pl.BlockSpec((None, tq, DA_HEAD_DIM), lambda b, h, i, f: (b, i, h)),
            chunk(ML_WIDTH, 0),
        ],
        scratch_shapes=[
            pltpu.VMEM((DA_HEAD_DIM, 2 * tq), BF16),
            pltpu.VMEM((tk, 2 * tq), F32),
            pltpu.VMEM((tk, 2 * tq), F32),
            pltpu.VMEM((tk, 2 * tq), BF16),
            pltpu.VMEM((tk, 2 * tq), BF16),
            pltpu.VMEM((1, 2 * tq), F32),
            pltpu.VMEM((1, 2 * tq), F32),
            pltpu.VMEM((1, 2 * tq), F32),
            pltpu.VMEM((1, 2 * tq), F32),
            pltpu.VMEM((1, 2 * tq), F32),
            pltpu.VMEM((1, 2 * tq), F32),
            pltpu.VMEM((DA_HEAD_DIM, 2 * tq), F32),
            pltpu.VMEM((ML_HEADS, ML_QK_DIM, ML_V_DIM), F32),
            pltpu.VMEM((ML_HEADS, 1, ML_QK_DIM), F32),
            pltpu.VMEM((ML_HEADS, 1, 1), F32),
            pltpu.VMEM((16, 2 * ML_QK_WIDTH), F32),
            pltpu.VMEM((CONV_WIDTH + 1, L, L), BF16),
            pltpu.VMEM((L, L), F32),
        ],
    )
    return pl.pallas_call(
        functools.partial(_attn_body, tq=tq, tk=tk),
        grid_spec=grid_spec,
        out_shape=[jax.ShapeDtypeStruct((B, S, DA_WIDTH), BF16),
                   jax.ShapeDtypeStruct((B, S, ML_WIDTH), BF16)],
        compiler_params=_cparams(("parallel", "arbitrary", "arbitrary")),
        name="diff_attn_mlstm",
    )(bounded, lam4, qt, kn, vt, g_col, proj3, proj3, proj3, proj3, gates3, gates_t,
      conv_w, conv_b, bias_row, bias_col, out_g)


def _outproj_body(x_ref, d_ref, m_ref, wd_ref, wm_ref, o_ref):
    o_ref[...] = (x_ref[...]
                  + jnp.dot(d_ref[...], wd_ref[...], preferred_element_type=F32)
                  + jnp.dot(m_ref[...], wm_ref[...], preferred_element_type=F32))


def _outproj(x2, d2, m2, w_out, bm):
    T = x2.shape[0]
    return pl.pallas_call(
        _outproj_body,
        grid=(T // bm,),
        in_specs=[
            pl.BlockSpec((bm, D_MODEL), lambda m: (m, 0)),
            pl.BlockSpec((bm, DA_WIDTH), lambda m: (m, 0)),
            pl.BlockSpec((bm, ML_WIDTH), lambda m: (m, 0)),
            pl.BlockSpec((DA_WIDTH, D_MODEL), lambda m: (0, 0)),
            pl.BlockSpec((ML_WIDTH, D_MODEL), lambda m: (1, 0)),
        ],
        out_specs=pl.BlockSpec((bm, D_MODEL), lambda m: (m, 0)),
        out_shape=jax.ShapeDtypeStruct((T, D_MODEL), F32),
        compiler_params=_cparams(("parallel",)),
        name="outproj",
    )(x2, d2, m2, w_out, w_out)


def _mlp_body(x_ref, g_ref, wu_ref, wd_ref, o_ref, h_ref):
    @pl.when(pl.program_id(1) == 0)
    def _():
        x = x_ref[...]
        ms = jnp.mean(x * x, axis=-1, keepdims=True)
        h_ref[...] = (x * lax.rsqrt(ms + NORM_EPS) * g_ref[...]).astype(BF16)
        o_ref[...] = x

    u = jnp.dot(h_ref[...], wu_ref[...], preferred_element_type=F32)
    a = jnp.square(jnp.maximum(u, 0.0)).astype(BF16)
    o_ref[...] += jnp.dot(a, wd_ref[...], preferred_element_type=F32)


def _mlp(x1, g, w_up, w_down, bm, tf):
    T = x1.shape[0]
    return pl.pallas_call(
        _mlp_body,
        grid=(T // bm, D_FF // tf),
        in_specs=[
            pl.BlockSpec((bm, D_MODEL), lambda m, f: (m, 0)),
            pl.BlockSpec((1, D_MODEL), lambda m, f: (0, 0)),
            pl.BlockSpec((D_MODEL, tf), lambda m, f: (0, f)),
            pl.BlockSpec((tf, D_MODEL), lambda m, f: (f, 0)),
        ],
        out_specs=pl.BlockSpec((bm, D_MODEL), lambda m, f: (m, 0)),
        out_shape=jax.ShapeDtypeStruct((T, D_MODEL), F32),
        scratch_shapes=[pltpu.VMEM((bm, D_MODEL), BF16)],
        compiler_params=_cparams(("parallel", "arbitrary")),
        name="mlp",
    )(x1, g, w_up, w_down)


def _tiles(B, S):
    T = B * S
    return dict(
        bm_in=min(1024, T), bn_in=1024,
        tp=min(512, S // 2),
        tq=min(1024, S),
        bm_out=min(512, T),
        bm_mlp=min(512, T), tf=1024,
    )


def kernel(x, norm1_g, w_in, ml_conv_w, ml_conv_b, ml_b_i, ml_b_f, ml_out_g, da_q_norm_g, da_k_norm_g, da_lambda_q1, da_lambda_k1, da_lambda_q2, da_lambda_k2, da_out_g, w_out, norm2_g, w_up, w_down):
    B, S, D = x.shape
    assert D == D_MODEL and norm1_g.shape[0] == 1, "single-layer kernel"
    t = _tiles(B, S)
    T = B * S
    x2 = x.reshape(T, D)

    w_in0 = w_in[0]
    w_main = w_in0.astype(BF16)
    w_gate = jnp.pad(w_in0[:, D_MAIN:], ((0, 0), (0, LANES - N_GATES))).astype(BF16)
    reps = STRIP // DA_QK_DIM
    gq2 = jnp.tile(da_q_norm_g[0], reps).reshape(1, STRIP) * (DA_QK_DIM ** -0.5 * math.log2(math.e))
    gk2 = jnp.tile(da_k_norm_g[0], reps).reshape(1, STRIP)
    score_bound = DA_QK_DIM * jnp.max(jnp.abs(gq2)) * jnp.max(jnp.abs(gk2))
    bounded = (score_bound <= SCORE_BOUND).astype(jnp.int32).reshape(1)
    lam4 = jnp.stack([da_lambda_q1[0], da_lambda_k1[0], da_lambda_q2[0], da_lambda_k2[0]])
    bias8 = jnp.concatenate([ml_b_i[0], ml_b_f[0]])
    bias_row = jnp.pad(bias8, (0, LANES - N_GATES)).reshape(1, LANES)
    bias_col = bias8.reshape(N_GATES, 1)

    proj, gates = _inproj(x2, norm1_g, w_main, w_gate, t["bm_in"], t["bn_in"])
    proj3 = proj.reshape(B, S, D_MAIN)
    gates3 = gates.reshape(B, S, LANES)
    gates_t = jnp.transpose(gates3[:, :, :N_GATES], (0, 2, 1))

    qt, kn, vt = _prep(proj3, gq2, gk2, t["tp"])
    d_out, m_out = _attn_mlstm(bounded, lam4, qt, kn, vt, da_out_g[0].reshape(DA_HEAD_DIM, 1),
                               proj3, gates3, gates_t, ml_conv_w[0], ml_conv_b, bias_row, bias_col,
                               ml_out_g[0], t["tq"], t["tp"])

    x1 = _outproj(x2, d_out.reshape(T, DA_WIDTH), m_out.reshape(T, ML_WIDTH),
                  w_out[0].astype(BF16), t["bm_out"])
    y = _mlp(x1, norm2_g, w_up[0].astype(BF16), w_down[0].astype(BF16), t["bm_mlp"], t["tf"])
    return y.reshape(B, S, D)
```

```python
import functools
import math

import jax
import jax.numpy as jnp
from jax import lax
from jax.experimental import pallas as pl
from jax.experimental.pallas import tpu as pltpu

F32 = jnp.float32
BF16 = jnp.bfloat16

D_MODEL = 2048
DA_HEADS = 8
DA_HEAD_DIM = 128
DA_QK_DIM = 64
DA_WIDTH = DA_HEADS * DA_HEAD_DIM
ML_HEADS = 4
ML_V_DIM = 256
ML_QK_DIM = 128
ML_WIDTH = ML_HEADS * ML_V_DIM
ML_QK_WIDTH = ML_HEADS * ML_QK_DIM
CONV_WIDTH = 4
D_FF = 4 * D_MODEL
D_MAIN = 3 * DA_WIDTH + 2 * ML_QK_WIDTH + 2 * ML_WIDTH
N_GATES = 2 * ML_HEADS
NORM_EPS = 1e-6
LAMBDA_INIT = 0.8 - 0.6 * math.exp(-0.3 * 0)
LANES = 128
STRIP = 256
NEG_BIG = -1e30
SCORE_BOUND = 60.0
VMEM_LIMIT = 56 * 1024 * 1024

COL_DA_Q, COL_DA_K, COL_DA_V = 0, 1, 2
COL_ML_Q, COL_ML_K = 6, 7
COL_ML_V, COL_ML_O = 4, 5


def _cparams(sem):
    return pltpu.CompilerParams(dimension_semantics=sem, vmem_limit_bytes=VMEM_LIMIT)


def _inproj_body(x_ref, g_ref, w_ref, wg_ref, o_ref, og_ref, h_ref):
    @pl.when(pl.program_id(1) == 0)
    def _():
        x = x_ref[...]
        ms = jnp.mean(x * x, axis=-1, keepdims=True)
        hb = (x * lax.rsqrt(ms + NORM_EPS) * g_ref[...]).astype(BF16)
        h_ref[...] = hb
        og_ref[...] = jnp.dot(hb, wg_ref[...], preferred_element_type=F32)

    o_ref[...] = jnp.dot(h_ref[...], w_ref[...], preferred_element_type=F32).astype(o_ref.dtype)


def _inproj(x2, g, w_main, w_gate, bm, bn):
    T = x2.shape[0]
    return pl.pallas_call(
        _inproj_body,
        grid=(T // bm, D_MAIN // bn),
        in_specs=[
            pl.BlockSpec((bm, D_MODEL), lambda m, n: (m, 0)),
            pl.BlockSpec((1, D_MODEL), lambda m, n: (0, 0)),
            pl.BlockSpec((D_MODEL, bn), lambda m, n: (0, n)),
            pl.BlockSpec((D_MODEL, LANES), lambda m, n: (0, 0)),
        ],
        out_specs=[
            pl.BlockSpec((bm, bn), lambda m, n: (m, n)),
            pl.BlockSpec((bm, LANES), lambda m, n: (m, 0)),
        ],
        out_shape=[
            jax.ShapeDtypeStruct((T, D_MAIN), BF16),
            jax.ShapeDtypeStruct((T, LANES), F32),
        ],
        scratch_shapes=[pltpu.VMEM((bm, D_MODEL), BF16)],
        compiler_params=_cparams(("parallel", "arbitrary")),
        name="inproj",
    )(x2, g, w_main, w_gate)


def _prep_body(q_ref, k_ref, v_ref, gq_ref, gk_ref, qt_ref, kn_ref, vt_ref):
    r = lax.broadcasted_iota(jnp.int32, (STRIP, STRIP), 0) // DA_QK_DIM
    c = lax.broadcasted_iota(jnp.int32, (STRIP, STRIP), 1) // DA_QK_DIM
    group = jnp.where(r == c, 1.0 / DA_QK_DIM, 0.0).astype(BF16)

    def norm(x, g):
        ms = jnp.dot((x * x).astype(BF16), group, preferred_element_type=F32)
        return x * lax.rsqrt(ms + NORM_EPS) * g

    for hp in range(DA_WIDTH // STRIP):
        sl = slice(hp * STRIP, (hp + 1) * STRIP)
        qn = norm(q_ref[:, sl].astype(F32), gq_ref[...])
        kn_ref[:, sl] = norm(k_ref[:, sl].astype(F32), gk_ref[...]).astype(BF16)
        v = v_ref[:, sl].astype(F32)
        for j in range(STRIP // DA_HEAD_DIM):
            h = hp * (STRIP // DA_HEAD_DIM) + j
            hs = slice(j * DA_HEAD_DIM, (j + 1) * DA_HEAD_DIM)
            qt_ref[h] = qn[:, hs].T.astype(BF16)
            vt_ref[h] = v[:, hs].T.astype(BF16)


def _prep(proj3, gq2, gk2, tp):
    B, S, _ = proj3.shape
    nk = S // tp
    blk = lambda col: pl.BlockSpec((None, tp, DA_WIDTH), lambda b, i, col=col: (b, i, col))
    vec = pl.BlockSpec((1, STRIP), lambda b, i: (0, 0))
    transposed = pl.BlockSpec((None, DA_HEADS, None, DA_HEAD_DIM, tp), lambda b, i: (b, 0, i, 0, 0))
    return pl.pallas_call(
        _prep_body,
        grid=(B, nk),
        in_specs=[blk(COL_DA_Q), blk(COL_DA_K), blk(COL_DA_V), vec, vec],
        out_specs=[
            transposed,
            pl.BlockSpec((None, tp, DA_WIDTH), lambda b, i: (b, i, 0)),
            transposed,
        ],
        out_shape=[
            jax.ShapeDtypeStruct((B, DA_HEADS, nk, DA_HEAD_DIM, tp), BF16),
            jax.ShapeDtypeStruct((B, S, DA_WIDTH), BF16),
            jax.ShapeDtypeStruct((B, DA_HEADS, nk, DA_HEAD_DIM, tp), BF16),
        ],
        compiler_params=_cparams(("parallel", "parallel")),
        name="attn_prep",
    )(proj3, proj3, proj3, gq2, gk2)


def _log_sigmoid(x):
    return jnp.minimum(x, 0.0) - jnp.log1p(jnp.exp(-jnp.abs(x)))


def _split3(x):
    hi = x.astype(BF16)
    r1 = x - hi.astype(F32)
    mid = r1.astype(BF16)
    lo = (r1 - mid.astype(F32)).astype(BF16)
    return hi, mid, lo


def _mlstm_gates(gc_ref, gr_ref, brow_ref, bcol_ref, tri_sc, gcol_sc, grow_sc):
    gc = gc_ref[...] + brow_ref[...]
    gr = gr_ref[...] + bcol_ref[...]
    gcol_sc[0] = gc
    grow_sc[0] = gr
    gcol_sc[1] = sum(jnp.dot(tri_sc[0], part, preferred_element_type=F32)
                     for part in _split3(_log_sigmoid(gc)))
    grow_sc[1] = sum(jnp.dot(part, tri_sc[1], preferred_element_type=F32)
                     for part in _split3(_log_sigmoid(gr)))


def _mlstm_reset(c_sc, n_sc, m_sc, tail_sc, tri_sc, bias_sc, *, L):
    c_sc[...] = jnp.zeros(c_sc.shape, F32)
    n_sc[...] = jnp.zeros(n_sc.shape, F32)
    m_sc[...] = jnp.zeros(m_sc.shape, F32)
    tail_sc[...] = jnp.zeros(tail_sc.shape, F32)
    ti = lax.broadcasted_iota(jnp.int32, (L, L), 0)
    si = lax.broadcasted_iota(jnp.int32, (L, L), 1)
    tri_sc[0] = (si <= ti).astype(BF16)
    tri_sc[1] = (ti <= si).astype(BF16)
    for j in range(1, CONV_WIDTH):
        tri_sc[1 + j] = (si == ti - j).astype(BF16)
    bias_sc[...] = jnp.where(si <= ti, 0.0, NEG_BIG)


def _mlstm_stages(mq_ref, mk_ref, mv_ref, mo_ref, gc_ref, gr_ref, gcn_ref, grn_ref, cw_ref, cb_ref,
                  brow_ref, bcol_ref, og_ref, out_ref, c_sc, n_sc, m_sc, tail_sc, tri_sc, bias_sc,
                  gcol_sc, grow_sc, *, L):
    gc, b_cols = gcol_sc[0], gcol_sc[1]
    gr, b_rows = grow_sc[0], grow_sc[1]
    _mlstm_gates(gcn_ref, grn_ref, brow_ref, bcol_ref, tri_sc, gcol_sc, grow_sc)
    yield

    qk = []
    for src_ref in (mq_ref, mk_ref):
        for c in range(ML_QK_WIDTH // STRIP):
            sl = slice(c * STRIP, (c + 1) * STRIP)
            dst = slice(len(qk) * STRIP, (len(qk) + 1) * STRIP)
            xb = src_ref[:, sl]
            x = xb.astype(F32)
            y = cb_ref[:, dst] + cw_ref[CONV_WIDTH - 1:CONV_WIDTH, dst] * x
            fix = jnp.zeros((8, STRIP), F32)
            for j in range(1, CONV_WIDTH):
                w = cw_ref[CONV_WIDTH - 1 - j:CONV_WIDTH - j, dst]
                y = y + w * jnp.dot(tri_sc[1 + j], xb, preferred_element_type=F32)
                fix = fix + w * tail_sc[8 - j:16 - j, dst]
            y = jnp.concatenate([y[:8] + fix, y[8:]], axis=0)
            tail_sc[0:8, dst] = x[L - 8:L, :]
            half = 0.5 * y
            qk.append(half + half * jnp.tanh(half))
            yield
    per_strip = STRIP // ML_QK_DIM
    head_cols = lambda j: qk[j // per_strip][:, (j % per_strip) * ML_QK_DIM:(j % per_strip + 1) * ML_QK_DIM]

    heads = range(ML_HEADS)
    q = [head_cols(h) * (ML_QK_DIM ** -0.5) for h in heads]
    k = [head_cols(ML_HEADS + h) for h in heads]
    v = [mv_ref[:, h * ML_V_DIM:(h + 1) * ML_V_DIM] for h in heads]
    i_col = [gc[:, h:h + 1] for h in heads]
    i_row = [gr[h:h + 1, :] for h in heads]
    b_col = [b_cols[:, ML_HEADS + h:ML_HEADS + h + 1] for h in heads]
    b_row = [b_rows[ML_HEADS + h:ML_HEADS + h + 1, :] for h in heads]
    num, den, m_t = {}, {}, {}

    for h in heads:
        m_prev = m_sc[h]
        log_inter = b_col[h] + m_prev
        dmat = b_col[h] - b_row[h] + i_row[h] + bias_sc[...]
        m_t[h] = jnp.maximum(log_inter, jnp.max(dmat, axis=1, keepdims=True))
        inter_w = jnp.exp(log_inter - m_t[h])
        qb = q[h].astype(BF16)
        s_qk = lax.dot_general(qb, k[h].astype(BF16), (((1,), (1,)), ((), ())),
                               preferred_element_type=F32)
        p = jnp.exp(dmat - m_t[h]) * s_qk
        num[h] = (inter_w * jnp.dot(qb, c_sc[h].astype(BF16), preferred_element_type=F32)
                  + jnp.dot(p.astype(BF16), v[h], preferred_element_type=F32))
        den[h] = (inter_w * jnp.sum(q[h] * n_sc[h], axis=1, keepdims=True)
                  + jnp.sum(p, axis=1, keepdims=True))
        yield

    for h in heads:
        hh = num[h] * (1.0 / jnp.maximum(jnp.abs(den[h]), jnp.exp(-m_t[h])))
        ms = jnp.mean(hh * hh, axis=1, keepdims=True)
        yh = hh * lax.rsqrt(ms + NORM_EPS) * og_ref[h:h + 1, :]
        o_half = 0.5 * mo_ref[:, h * ML_V_DIM:(h + 1) * ML_V_DIM].astype(F32)
        gate = 0.5 + 0.5 * jnp.tanh(o_half)
        out_ref[:, h * ML_V_DIM:(h + 1) * ML_V_DIM] = (yh * gate).astype(out_ref.dtype)
        yield

    for h in heads:
        m_prev = m_sc[h]
        a = b_col[h][L - 1:L, :]
        g_col = a - b_col[h] + i_col[h]
        g_max = jnp.max(g_col, axis=0, keepdims=True)
        kw = k[h] * jnp.exp(g_col - g_max)
        c_loc = lax.dot_general(kw.astype(BF16), v[h], (((0,), (0,)), ((), ())),
                                preferred_element_type=F32)
        n_loc = jnp.sum(kw, axis=0, keepdims=True)
        m_new = jnp.maximum(a + m_prev, g_max)
        decay = jnp.exp(a + m_prev - m_new)
        scale = jnp.exp(g_max - m_new)
        c_sc[h] = decay * c_sc[h] + scale * c_loc
        n_sc[h] = decay * n_sc[h] + scale * n_loc
        m_sc[h] = m_new
        yield


def _interleave(*stage_generators):
    live = list(stage_generators)
    while live:
        live = [g for g in live if next(g, StopIteration) is not StopIteration]


N_ATTN_IN, N_MLSTM_IN, N_ATTN_SCRATCH = 5, 13, 12


def _attn_body(bounded_ref, *refs, tq, tk):
    lam_ref, q_ref, k_ref, vt_ref, g_ref = refs[:N_ATTN_IN]
    ml_in = refs[N_ATTN_IN:N_ATTN_IN + N_MLSTM_IN]
    o_ref, ml_out = refs[N_ATTN_IN + N_MLSTM_IN:N_ATTN_IN + N_MLSTM_IN + 2]
    scratch = refs[N_ATTN_IN + N_MLSTM_IN + 2:]
    (qq_sc, s0_sc, s1_sc, p0_sc, p1_sc, a0_sc, a1_sc, c0_sc, c1_sc, m_sc, l_sc,
     acc_sc) = scratch[:N_ATTN_SCRATCH]
    ml_state = scratch[N_ATTN_SCRATCH:]
    s_sc, p_sc, a_sc, c_sc = (s0_sc, s1_sc), (p0_sc, p1_sc), (a0_sc, a1_sc), (c0_sc, c1_sc)
    qi = pl.program_id(2)

    @pl.when((pl.program_id(1) == 0) & (qi == 0))
    def _():
        (_, _, _, _, gc_ref, gr_ref, _, _, _, _, brow_ref, bcol_ref, _) = ml_in
        (ml_c, ml_n, ml_m, tail_sc, tri_sc, bias_sc, gcol_sc, grow_sc) = ml_state
        _mlstm_reset(ml_c, ml_n, ml_m, tail_sc, tri_sc, bias_sc, L=tq // DA_HEADS)
        _mlstm_gates(gc_ref, gr_ref, brow_ref, bcol_ref, tri_sc, gcol_sc, grow_sc)

    def mlstm_stages():
        return _mlstm_stages(*ml_in, ml_out, *ml_state, L=tq // DA_HEADS)

    row = lax.broadcasted_iota(jnp.int32, (DA_HEAD_DIM, tk), 0)
    for c in range(tq // tk):
        qt = q_ref[c]
        zero = jnp.zeros_like(qt)
        qq_sc[:, c * tk:(c + 1) * tk] = jnp.where(row < DA_QK_DIM, qt, zero)
        qq_sc[:, tq + c * tk:tq + (c + 1) * tk] = jnp.where(row >= DA_QK_DIM, qt, zero)

    l_sc[...] = jnp.zeros(l_sc.shape, F32)
    acc_sc[...] = jnp.zeros(acc_sc.shape, F32)

    strips = [slice(c * STRIP, (c + 1) * STRIP) for c in range(2 * tq // STRIP)]

    def key_chunk(t):
        return k_ref[pl.ds(pl.multiple_of(t * tk, tk), tk), :]

    def causal(diag, sl):
        q_lo = sl.start % tq
        if q_lo >= (diag + 1) * tk - 1:
            return None
        kpos = diag * tk + lax.broadcasted_iota(jnp.int32, (tk, STRIP), 0)
        qpos = q_lo + lax.broadcasted_iota(jnp.int32, (tk, STRIP), 1)
        return kpos <= qpos

    def visible(diag, sl):
        return (sl.start % tq) + STRIP - 1 >= diag * tk

    def probs_stages(t, par, diag=None):
        kj = key_chunk(t)
        for sl in strips:
            if diag is not None and not visible(diag, sl):
                continue
            p = jnp.exp2(jnp.dot(kj, qq_sc[:, sl], preferred_element_type=F32))
            mask = None if diag is None else causal(diag, sl)
            if mask is not None:
                p = jnp.where(mask, p, 0.0)
            l_sc[:, sl] += jnp.sum(p, axis=0, keepdims=True)
            p_sc[par][:, sl] = p.astype(BF16)
            yield

    def values_stages(t, par, diag=None):
        vt = vt_ref[t]
        for sl in strips:
            if diag is not None and not visible(diag, sl):
                continue
            acc_sc[:, sl] += jnp.dot(vt, p_sc[par][:, sl], preferred_element_type=F32)
            yield

    def probs(*args, **kwargs):
        _interleave(probs_stages(*args, **kwargs))

    def values_plain(*args, **kwargs):
        _interleave(values_stages(*args, **kwargs))

    def chain(*gens):
        for g in gens:
            yield from g

    def finalize_stages():
        lv = lam_ref[...]
        lam = (jnp.exp(jnp.sum(lv[0:1] * lv[1:2], axis=-1, keepdims=True))
               - jnp.exp(jnp.sum(lv[2:3] * lv[3:4], axis=-1, keepdims=True)) + LAMBDA_INIT)
        for c in range(tq // STRIP):
            m1 = slice(c * STRIP, (c + 1) * STRIP)
            m2 = slice(tq + c * STRIP, tq + (c + 1) * STRIP)
            o = acc_sc[:, m1] / l_sc[:, m1] - lam * (acc_sc[:, m2] / l_sc[:, m2])
            ms = jnp.mean(o * o, axis=0, keepdims=True)
            y = o * lax.rsqrt(ms + NORM_EPS) * g_ref[...] * (1.0 - LAMBDA_INIT)
            o_ref[m1, :] = y.T.astype(o_ref.dtype)
            yield

    def bounded_path():
        def pair(t):
            probs(t, 0)
            probs(t + 1, 1)
            values_plain(t, 0)
            values_plain(t + 1, 1)

        def two_pairs(i, carry):
            pair(4 * i)
            pair(4 * i + 2)
            return carry

        lax.fori_loop(0, lax.shift_right_logical(qi, 1), two_pairs, 0)

        @pl.when((qi & 1) == 1)
        def _():
            pair(2 * qi - 2)

        d0 = 2 * qi
        _interleave(chain(probs_stages(d0, 0, diag=0), probs_stages(d0 + 1, 1, diag=1),
                          values_stages(d0, 0, diag=0), values_stages(d0 + 1, 1, diag=1),
                          finalize_stages()),
                    mlstm_stages())

    def scores(t, par):
        kj = key_chunk(t)
        for sl in strips:
            s = jnp.dot(kj, qq_sc[:, sl], preferred_element_type=F32)
            s_sc[par][:, sl] = s
            c_sc[par][:, sl] = jnp.max(s, axis=0, keepdims=True)

    def softmax(par, diag):
        for sl in strips:
            s = s_sc[par][:, sl]
            mask = None if diag is None else causal(diag, sl)
            if mask is None:
                cmax = c_sc[par][:, sl]
            else:
                s = jnp.where(mask, s, NEG_BIG)
                cmax = jnp.max(s, axis=0, keepdims=True)
            m_old = m_sc[:, sl]
            m_new = jnp.maximum(m_old, cmax)
            p = jnp.exp2(s - m_new)
            alpha = jnp.exp2(m_old - m_new)
            l_sc[:, sl] = alpha * l_sc[:, sl] + jnp.sum(p, axis=0, keepdims=True)
            m_sc[:, sl] = m_new
            a_sc[par][:, sl] = alpha
            p_sc[par][:, sl] = p.astype(BF16)

    def values(t, par):
        vt = vt_ref[t]
        for sl in strips:
            acc_sc[:, sl] = a_sc[par][:, sl] * acc_sc[:, sl] + jnp.dot(
                vt, p_sc[par][:, sl], preferred_element_type=F32)

    def online_path():
        m_sc[...] = jnp.full(m_sc.shape, NEG_BIG, F32)
        p1_sc[...] = jnp.zeros(p1_sc.shape, BF16)
        a1_sc[...] = jnp.ones(a1_sc.shape, F32)
        scores(0, 0)
        _interleave(mlstm_stages())

        def pair(i, carry):
            t = 2 * i
            scores(t + 1, 1)
            softmax(0, None)
            values(jnp.maximum(t - 1, 0), 1)
            scores(t + 2, 0)
            softmax(1, None)
            values(t, 0)
            return carry

        lax.fori_loop(0, qi, pair, 0)
        t = 2 * qi
        scores(t + 1, 1)
        softmax(0, 0)
        values(jnp.maximum(t - 1, 0), 1)
        softmax(1, 1)
        values(t, 0)
        values(t + 1, 1)
        _interleave(finalize_stages())

    pl.when(bounded_ref[0] == 1)(bounded_path)
    pl.when(bounded_ref[0] != 1)(online_path)


def _attn_mlstm(bounded, lam4, qt, kn, vt, g_col, proj3, gates3, gates_t, conv_w, conv_b, bias_row,
                bias_col, out_g, tq, tk):
    B, S, _ = kn.shape
    nk, nq = S // tk, S // tq
    L = tq // DA_HEADS
    assert tq == 2 * tk, "a query block spans exactly two key chunks"
    assert L % 8 == 0 and L * DA_HEADS * nq == S
    full = lambda shape: pl.BlockSpec(shape, lambda b, h, i, f: (0,) * len(shape))
    chunk = lambda width, col: pl.BlockSpec((None, L, width), lambda b, h, i, f: (b, h * nq + i, col))
    nxt = lambda h, i: jnp.minimum(h * nq + i + 1, DA_HEADS * nq - 1)
    grid_spec = pltpu.PrefetchScalarGridSpec(
        num_scalar_prefetch=1,
        grid=(B, DA_HEADS, nq),
        in_specs=[
            full((4, DA_QK_DIM)),
            pl.BlockSpec((None, None, tq // tk, DA_HEAD_DIM, tk), lambda b, h, i, f: (b, h, i, 0, 0)),
            pl.BlockSpec((None, S, DA_HEAD_DIM), lambda b, h, i, f: (b, 0, h)),
            pl.BlockSpec((None, None, nk, DA_HEAD_DIM, tk), lambda b, h, i, f: (b, h, 0, 0, 0)),
            full((DA_HEAD_DIM, 1)),
            chunk(ML_QK_WIDTH, COL_ML_Q),
            chunk(ML_QK_WIDTH, COL_ML_K),
            chunk(ML_WIDTH, COL_ML_V),
            chunk(ML_WIDTH, COL_ML_O),
            chunk(LANES, 0),
            pl.BlockSpec((None, N_GATES, L), lambda b, h, i, f: (b, 0, h * nq + i)),
            pl.BlockSpec((None, L, LANES), lambda b, h, i, f: (b, nxt(h, i), 0)),
            pl.BlockSpec((None, N_GATES, L), lambda b, h, i, f: (b, 0, nxt(h, i))),
            full((CONV_WIDTH, 2 * ML_QK_WIDTH)),
            full((1, 2 * ML_QK_WIDTH)),
            full((1, LANES)),
            full((N_GATES, 1)),
            full((ML_HEADS, ML_V_DIM)),
        ],
        out_specs=[
            pl.BlockSpec((None, tq, DA_HEAD_DIM), lambda b, h, i, f: (b, i, h)),
            chunk(ML_WIDTH, 0),
        ],
        scratch_shapes=[
            pltpu.VMEM((DA_HEAD_DIM, 2 * tq), BF16),
            pltpu.VMEM((tk, 2 * tq), F32),
            pltpu.VMEM((tk, 2 * tq), F32),
            pltpu.VMEM((tk, 2 * tq), BF16),
            pltpu.VMEM((tk, 2 * tq), BF16),
            pltpu.VMEM((1, 2 * tq), F32),
            pltpu.VMEM((1, 2 * tq), F32),
            pltpu.VMEM((1, 2 * tq), F32),
            pltpu.VMEM((1, 2 * tq), F32),
            pltpu.VMEM((1, 2 * tq), F32),
            pltpu.VMEM((1, 2 * tq), F32),
            pltpu.VMEM((DA_HEAD_DIM, 2 * tq), F32),
            pltpu.VMEM((ML_HEADS, ML_QK_DIM, ML_V_DIM), F32),
            pltpu.VMEM((ML_HEADS, 1, ML_QK_DIM), F32),
            pltpu.VMEM((ML_HEADS, 1, 1), F32),
            pltpu.VMEM((16, 2 * ML_QK_WIDTH), F32),
            pltpu.VMEM((CONV_WIDTH + 1, L, L), BF16),
            pltpu.VMEM((L, L), F32),
            pltpu.VMEM((2, L, LANES), F32),
            pltpu.VMEM((2, N_GATES, L), F32),
        ],
    )
    return pl.pallas_call(
        functools.partial(_attn_body, tq=tq, tk=tk),
        grid_spec=grid_spec,
        out_shape=[jax.ShapeDtypeStruct((B, S, DA_WIDTH), BF16),
                   jax.ShapeDtypeStruct((B, S, ML_WIDTH), BF16)],
        compiler_params=_cparams(("parallel", "arbitrary", "arbitrary")),
        name="diff_attn_mlstm",
    )(bounded, lam4, qt, kn, vt, g_col, proj3, proj3, proj3, proj3, gates3, gates_t, gates3, gates_t,
      conv_w, conv_b, bias_row, bias_col, out_g)


def _outproj_body(x_ref, d_ref, m_ref, wd_ref, wm_ref, o_ref):
    o_ref[...] = (x_ref[...]
                  + jnp.dot(d_ref[...], wd_ref[...], preferred_element_type=F32)
                  + jnp.dot(m_ref[...], wm_ref[...], preferred_element_type=F32))


def _outproj(x2, d2, m2, w_out, bm):
    T = x2.shape[0]
    return pl.pallas_call(
        _outproj_body,
        grid=(T // bm,),
        in_specs=[
            pl.BlockSpec((bm, D_MODEL), lambda m: (m, 0)),
            pl.BlockSpec((bm, DA_WIDTH), lambda m: (m, 0)),
            pl.BlockSpec((bm, ML_WIDTH), lambda m: (m, 0)),
            pl.BlockSpec((DA_WIDTH, D_MODEL), lambda m: (0, 0)),
            pl.BlockSpec((ML_WIDTH, D_MODEL), lambda m: (1, 0)),
        ],
        out_specs=pl.BlockSpec((bm, D_MODEL), lambda m: (m, 0)),
        out_shape=jax.ShapeDtypeStruct((T, D_MODEL), F32),
        compiler_params=_cparams(("parallel",)),
        name="outproj",
    )(x2, d2, m2, w_out, w_out)


def _mlp_body(x_ref, g_ref, wu_ref, wd_ref, o_ref, h_ref):
    @pl.when(pl.program_id(1) == 0)
    def _():
        x = x_ref[...]
        ms = jnp.mean(x * x, axis=-1, keepdims=True)
        h_ref[...] = (x * lax.rsqrt(ms + NORM_EPS) * g_ref[...]).astype(BF16)
        o_ref[...] = x

    u = jnp.dot(h_ref[...], wu_ref[...], preferred_element_type=F32)
    a = jnp.square(jnp.maximum(u, 0.0)).astype(BF16)
    o_ref[...] += jnp.dot(a, wd_ref[...], preferred_element_type=F32)


def _mlp(x1, g, w_up, w_down, bm, tf):
    T = x1.shape[0]
    return pl.pallas_call(
        _mlp_body,
        grid=(T // bm, D_FF // tf),
        in_specs=[
            pl.BlockSpec((bm, D_MODEL), lambda m, f: (m, 0)),
            pl.BlockSpec((1, D_MODEL), lambda m, f: (0, 0)),
            pl.BlockSpec((D_MODEL, tf), lambda m, f: (0, f)),
            pl.BlockSpec((tf, D_MODEL), lambda m, f: (f, 0)),
        ],
        out_specs=pl.BlockSpec((bm, D_MODEL), lambda m, f: (m, 0)),
        out_shape=jax.ShapeDtypeStruct((T, D_MODEL), F32),
        scratch_shapes=[pltpu.VMEM((bm, D_MODEL), BF16)],
        compiler_params=_cparams(("parallel", "arbitrary")),
        name="mlp",
    )(x1, g, w_up, w_down)


def _tiles(B, S):
    T = B * S
    return dict(
        bm_in=min(1024, T), bn_in=1024,
        tp=min(512, S // 2),
        tq=min(1024, S),
        bm_out=min(512, T),
        bm_mlp=min(512, T), tf=1024,
    )


def kernel(x, norm1_g, w_in, ml_conv_w, ml_conv_b, ml_b_i, ml_b_f, ml_out_g, da_q_norm_g, da_k_norm_g, da_lambda_q1, da_lambda_k1, da_lambda_q2, da_lambda_k2, da_out_g, w_out, norm2_g, w_up, w_down):
    B, S, D = x.shape
    assert D == D_MODEL and norm1_g.shape[0] == 1, "single-layer kernel"
    t = _tiles(B, S)
    T = B * S
    x2 = x.reshape(T, D)

    w_in0 = w_in[0]
    w_main = w_in0.astype(BF16)
    w_gate = jnp.pad(w_in0[:, D_MAIN:], ((0, 0), (0, LANES - N_GATES))).astype(BF16)
    reps = STRIP // DA_QK_DIM
    gq2 = jnp.tile(da_q_norm_g[0], reps).reshape(1, STRIP) * (DA_QK_DIM ** -0.5 * math.log2(math.e))
    gk2 = jnp.tile(da_k_norm_g[0], reps).reshape(1, STRIP)
    score_bound = DA_QK_DIM * jnp.max(jnp.abs(gq2)) * jnp.max(jnp.abs(gk2))
    bounded = (score_bound <= SCORE_BOUND).astype(jnp.int32).reshape(1)
    lam4 = jnp.stack([da_lambda_q1[0], da_lambda_k1[0], da_lambda_q2[0], da_lambda_k2[0]])
    bias8 = jnp.concatenate([ml_b_i[0], ml_b_f[0]])
    bias_row = jnp.pad(bias8, (0, LANES - N_GATES)).reshape(1, LANES)
    bias_col = bias8.reshape(N_GATES, 1)

    proj, gates = _inproj(x2, norm1_g, w_main, w_gate, t["bm_in"], t["bn_in"])
    proj3 = proj.reshape(B, S, D_MAIN)
    gates3 = gates.reshape(B, S, LANES)
    gates_t = jnp.transpose(gates3[:, :, :N_GATES], (0, 2, 1))

    qt, kn, vt = _prep(proj3, gq2, gk2, t["tp"])
    d_out, m_out = _attn_mlstm(bounded, lam4, qt, kn, vt, da_out_g[0].reshape(DA_HEAD_DIM, 1),
                               proj3, gates3, gates_t, ml_conv_w[0], ml_conv_b, bias_row, bias_col,
                               ml_out_g[0], t["tq"], t["tp"])

    x1 = _outproj(x2, d_out.reshape(T, DA_WIDTH), m_out.reshape(T, ML_WIDTH),
                  w_out[0].astype(BF16), t["bm_out"])
    y = _mlp(x1, norm2_g, w_up[0].astype(BF16), w_down[0].astype(BF16), t["bm_mlp"], t["tf"])
    return y.reshape(B, S, D)
```

```python
import functools
import math

import jax
import jax.numpy as jnp
from jax import lax
from jax.experimental import pallas as pl
from jax.experimental.pallas import tpu as pltpu

F32 = jnp.float32
BF16 = jnp.bfloat16

D_MODEL = 2048
DA_HEADS = 8
DA_HEAD_DIM = 128
DA_QK_DIM = 64
DA_WIDTH = DA_HEADS * DA_HEAD_DIM
ML_HEADS = 4
ML_V_DIM = 256
ML_QK_DIM = 128
ML_WIDTH = ML_HEADS * ML_V_DIM
ML_QK_WIDTH = ML_HEADS * ML_QK_DIM
CONV_WIDTH = 4
D_FF = 4 * D_MODEL
D_MAIN = 3 * DA_WIDTH + 2 * ML_QK_WIDTH + 2 * ML_WIDTH
N_GATES = 2 * ML_HEADS
NORM_EPS = 1e-6
LAMBDA_INIT = 0.8 - 0.6 * math.exp(-0.3 * 0)
LANES = 128
STRIP = 256
NEG_BIG = -1e30
SCORE_BOUND = 60.0
VMEM_LIMIT = 56 * 1024 * 1024

COL_DA_Q, COL_DA_K, COL_DA_V = 0, 1, 2
COL_ML_Q, COL_ML_K = 6, 7
COL_ML_V, COL_ML_O = 4, 5


def _cparams(sem):
    return pltpu.CompilerParams(dimension_semantics=sem, vmem_limit_bytes=VMEM_LIMIT)


def _inproj_body(x_ref, g_ref, w_ref, wg_ref, o_ref, og_ref, h_ref):
    @pl.when(pl.program_id(1) == 0)
    def _():
        x = x_ref[...]
        ms = jnp.mean(x * x, axis=-1, keepdims=True)
        hb = (x * lax.rsqrt(ms + NORM_EPS) * g_ref[...]).astype(BF16)
        h_ref[...] = hb
        og_ref[...] = jnp.dot(hb, wg_ref[...], preferred_element_type=F32)

    o_ref[...] = jnp.dot(h_ref[...], w_ref[...], preferred_element_type=F32).astype(o_ref.dtype)


def _inproj(x2, g, w_main, w_gate, bm, bn):
    T = x2.shape[0]
    return pl.pallas_call(
        _inproj_body,
        grid=(T // bm, D_MAIN // bn),
        in_specs=[
            pl.BlockSpec((bm, D_MODEL), lambda m, n: (m, 0)),
            pl.BlockSpec((1, D_MODEL), lambda m, n: (0, 0)),
            pl.BlockSpec((D_MODEL, bn), lambda m, n: (0, n)),
            pl.BlockSpec((D_MODEL, LANES), lambda m, n: (0, 0)),
        ],
        out_specs=[
            pl.BlockSpec((bm, bn), lambda m, n: (m, n)),
            pl.BlockSpec((bm, LANES), lambda m, n: (m, 0)),
        ],
        out_shape=[
            jax.ShapeDtypeStruct((T, D_MAIN), BF16),
            jax.ShapeDtypeStruct((T, LANES), F32),
        ],
        scratch_shapes=[pltpu.VMEM((bm, D_MODEL), BF16)],
        compiler_params=_cparams(("parallel", "arbitrary")),
        name="inproj",
    )(x2, g, w_main, w_gate)


def _prep_body(q_ref, k_ref, v_ref, gq_ref, gk_ref, qt_ref, kn_ref, vt_ref):
    r = lax.broadcasted_iota(jnp.int32, (STRIP, STRIP), 0) // DA_QK_DIM
    c = lax.broadcasted_iota(jnp.int32, (STRIP, STRIP), 1) // DA_QK_DIM
    group = jnp.where(r == c, 1.0 / DA_QK_DIM, 0.0).astype(BF16)

    def norm(x, g):
        ms = jnp.dot((x * x).astype(BF16), group, preferred_element_type=F32)
        return x * lax.rsqrt(ms + NORM_EPS) * g

    for hp in range(DA_WIDTH // STRIP):
        sl = slice(hp * STRIP, (hp + 1) * STRIP)
        qn = norm(q_ref[:, sl].astype(F32), gq_ref[...])
        kn_ref[:, sl] = norm(k_ref[:, sl].astype(F32), gk_ref[...]).astype(BF16)
        v = v_ref[:, sl].astype(F32)
        for j in range(STRIP // DA_HEAD_DIM):
            h = hp * (STRIP // DA_HEAD_DIM) + j
            hs = slice(j * DA_HEAD_DIM, (j + 1) * DA_HEAD_DIM)
            qt_ref[h] = qn[:, hs].T.astype(BF16)
            vt_ref[h] = v[:, hs].T.astype(BF16)


def _prep(proj3, gq2, gk2, tp):
    B, S, _ = proj3.shape
    nk = S // tp
    blk = lambda col: pl.BlockSpec((None, tp, DA_WIDTH), lambda b, i, col=col: (b, i, col))
    vec = pl.BlockSpec((1, STRIP), lambda b, i: (0, 0))
    transposed = pl.BlockSpec((None, DA_HEADS, None, DA_HEAD_DIM, tp), lambda b, i: (b, 0, i, 0, 0))
    return pl.pallas_call(
        _prep_body,
        grid=(B, nk),
        in_specs=[blk(COL_DA_Q), blk(COL_DA_K), blk(COL_DA_V), vec, vec],
        out_specs=[
            transposed,
            pl.BlockSpec((None, tp, DA_WIDTH), lambda b, i: (b, i, 0)),
            transposed,
        ],
        out_shape=[
            jax.ShapeDtypeStruct((B, DA_HEADS, nk, DA_HEAD_DIM, tp), BF16),
            jax.ShapeDtypeStruct((B, S, DA_WIDTH), BF16),
            jax.ShapeDtypeStruct((B, DA_HEADS, nk, DA_HEAD_DIM, tp), BF16),
        ],
        compiler_params=_cparams(("parallel", "parallel")),
        name="attn_prep",
    )(proj3, proj3, proj3, gq2, gk2)


def _log_sigmoid(x):
    return jnp.minimum(x, 0.0) - jnp.log1p(jnp.exp(-jnp.abs(x)))


def _split3(x):
    hi = x.astype(BF16)
    r1 = x - hi.astype(F32)
    mid = r1.astype(BF16)
    lo = (r1 - mid.astype(F32)).astype(BF16)
    return hi, mid, lo


def _mlstm_gates(gc_ref, gr_ref, brow_ref, bcol_ref, tri_sc, gcol_sc, grow_sc):
    gc = gc_ref[...] + brow_ref[...]
    gr = gr_ref[...] + bcol_ref[...]
    gcol_sc[0] = gc
    grow_sc[0] = gr
    gcol_sc[1] = sum(jnp.dot(tri_sc[0], part, preferred_element_type=F32)
                     for part in _split3(_log_sigmoid(gc)))
    grow_sc[1] = sum(jnp.dot(part, tri_sc[1], preferred_element_type=F32)
                     for part in _split3(_log_sigmoid(gr)))


def _mlstm_reset(c_sc, n_sc, m_sc, tail_sc, tri_sc, bias_sc, *, L):
    c_sc[...] = jnp.zeros(c_sc.shape, F32)
    n_sc[...] = jnp.zeros(n_sc.shape, F32)
    m_sc[...] = jnp.zeros(m_sc.shape, F32)
    tail_sc[...] = jnp.zeros(tail_sc.shape, F32)
    ti = lax.broadcasted_iota(jnp.int32, (L, L), 0)
    si = lax.broadcasted_iota(jnp.int32, (L, L), 1)
    tri_sc[0] = (si <= ti).astype(BF16)
    tri_sc[1] = (ti <= si).astype(BF16)
    for j in range(1, CONV_WIDTH):
        tri_sc[1 + j] = (si == ti - j).astype(BF16)
    bias_sc[...] = jnp.where(si <= ti, 0.0, NEG_BIG)


def _mlstm_stages(mq_ref, mk_ref, mv_ref, mo_ref, gc_ref, gr_ref, gcn_ref, grn_ref, cw_ref, cb_ref,
                  brow_ref, bcol_ref, og_ref, out_ref, c_sc, n_sc, m_sc, tail_sc, tri_sc, bias_sc,
                  gcol_sc, grow_sc, *, L):
    gc, b_cols = gcol_sc[0], gcol_sc[1]
    gr, b_rows = grow_sc[0], grow_sc[1]
    _mlstm_gates(gcn_ref, grn_ref, brow_ref, bcol_ref, tri_sc, gcol_sc, grow_sc)
    yield

    qk = []
    for src_ref in (mq_ref, mk_ref):
        for c in range(ML_QK_WIDTH // STRIP):
            sl = slice(c * STRIP, (c + 1) * STRIP)
            dst = slice(len(qk) * STRIP, (len(qk) + 1) * STRIP)
            xb = src_ref[:, sl]
            x = xb.astype(F32)
            y = cb_ref[:, dst] + cw_ref[CONV_WIDTH - 1:CONV_WIDTH, dst] * x
            fix = jnp.zeros((8, STRIP), F32)
            for j in range(1, CONV_WIDTH):
                w = cw_ref[CONV_WIDTH - 1 - j:CONV_WIDTH - j, dst]
                y = y + w * jnp.dot(tri_sc[1 + j], xb, preferred_element_type=F32)
                fix = fix + w * tail_sc[8 - j:16 - j, dst]
            y = jnp.concatenate([y[:8] + fix, y[8:]], axis=0)
            tail_sc[0:8, dst] = x[L - 8:L, :]
            half = 0.5 * y
            qk.append(half + half * jnp.tanh(half))
            yield
    per_strip = STRIP // ML_QK_DIM
    head_cols = lambda j: qk[j // per_strip][:, (j % per_strip) * ML_QK_DIM:(j % per_strip + 1) * ML_QK_DIM]

    heads = range(ML_HEADS)
    q = [head_cols(h) * (ML_QK_DIM ** -0.5) for h in heads]
    k = [head_cols(ML_HEADS + h) for h in heads]
    v = [mv_ref[:, h * ML_V_DIM:(h + 1) * ML_V_DIM] for h in heads]
    i_col = [gc[:, h:h + 1] for h in heads]
    i_row = [gr[h:h + 1, :] for h in heads]
    b_col = [b_cols[:, ML_HEADS + h:ML_HEADS + h + 1] for h in heads]
    b_row = [b_rows[ML_HEADS + h:ML_HEADS + h + 1, :] for h in heads]
    num, den, m_t = {}, {}, {}

    for h in heads:
        m_prev = m_sc[h]
        log_inter = b_col[h] + m_prev
        dmat = b_col[h] - b_row[h] + i_row[h] + bias_sc[...]
        m_t[h] = jnp.maximum(log_inter, jnp.max(dmat, axis=1, keepdims=True))
        inter_w = jnp.exp(log_inter - m_t[h])
        qb = q[h].astype(BF16)
        s_qk = lax.dot_general(qb, k[h].astype(BF16), (((1,), (1,)), ((), ())),
                               preferred_element_type=F32)
        p = jnp.exp(dmat - m_t[h]) * s_qk
        num[h] = (inter_w * jnp.dot(qb, c_sc[h].astype(BF16), preferred_element_type=F32)
                  + jnp.dot(p.astype(BF16), v[h], preferred_element_type=F32))
        den[h] = (inter_w * jnp.sum(q[h] * n_sc[h], axis=1, keepdims=True)
                  + jnp.sum(p, axis=1, keepdims=True))
        yield

    for h in heads:
        hh = num[h] * (1.0 / jnp.maximum(jnp.abs(den[h]), jnp.exp(-m_t[h])))
        ms = jnp.mean(hh * hh, axis=1, keepdims=True)
        yh = hh * lax.rsqrt(ms + NORM_EPS) * og_ref[h:h + 1, :]
        o_half = 0.5 * mo_ref[:, h * ML_V_DIM:(h + 1) * ML_V_DIM].astype(F32)
        gate = 0.5 + 0.5 * jnp.tanh(o_half)
        out_ref[:, h * ML_V_DIM:(h + 1) * ML_V_DIM] = (yh * gate).astype(out_ref.dtype)
        yield

    for h in heads:
        m_prev = m_sc[h]
        a = b_col[h][L - 1:L, :]
        g_col = a - b_col[h] + i_col[h]
        g_max = jnp.max(g_col, axis=0, keepdims=True)
        kw = k[h] * jnp.exp(g_col - g_max)
        c_loc = lax.dot_general(kw.astype(BF16), v[h], (((0,), (0,)), ((), ())),
                                preferred_element_type=F32)
        n_loc = jnp.sum(kw, axis=0, keepdims=True)
        m_new = jnp.maximum(a + m_prev, g_max)
        decay = jnp.exp(a + m_prev - m_new)
        scale = jnp.exp(g_max - m_new)
        c_sc[h] = decay * c_sc[h] + scale * c_loc
        n_sc[h] = decay * n_sc[h] + scale * n_loc
        m_sc[h] = m_new
        yield


def _interleave(*stage_generators):
    live = list(stage_generators)
    while live:
        live = [g for g in live if next(g, StopIteration) is not StopIteration]


N_ATTN_IN, N_MLSTM_IN, N_ATTN_SCRATCH = 5, 13, 12


def _attn_body(bounded_ref, *refs, tq, tk):
    lam_ref, q_ref, k_ref, vt_ref, g_ref = refs[:N_ATTN_IN]
    ml_in = refs[N_ATTN_IN:N_ATTN_IN + N_MLSTM_IN]
    o_ref, ml_out = refs[N_ATTN_IN + N_MLSTM_IN:N_ATTN_IN + N_MLSTM_IN + 2]
    scratch = refs[N_ATTN_IN + N_MLSTM_IN + 2:]
    (qq_sc, s0_sc, s1_sc, p0_sc, p1_sc, a0_sc, a1_sc, c0_sc, c1_sc, m_sc, l_sc,
     acc_sc) = scratch[:N_ATTN_SCRATCH]
    ml_state = scratch[N_ATTN_SCRATCH:]
    s_sc, p_sc, a_sc, c_sc = (s0_sc, s1_sc), (p0_sc, p1_sc), (a0_sc, a1_sc), (c0_sc, c1_sc)
    qi = pl.program_id(2)

    @pl.when((pl.program_id(1) == 0) & (qi == 0))
    def _():
        (_, _, _, _, gc_ref, gr_ref, _, _, _, _, brow_ref, bcol_ref, _) = ml_in
        (ml_c, ml_n, ml_m, tail_sc, tri_sc, bias_sc, gcol_sc, grow_sc) = ml_state
        _mlstm_reset(ml_c, ml_n, ml_m, tail_sc, tri_sc, bias_sc, L=tq // DA_HEADS)
        _mlstm_gates(gc_ref, gr_ref, brow_ref, bcol_ref, tri_sc, gcol_sc, grow_sc)

    def mlstm_stages():
        return _mlstm_stages(*ml_in, ml_out, *ml_state, L=tq // DA_HEADS)

    row = lax.broadcasted_iota(jnp.int32, (DA_HEAD_DIM, tk), 0)
    for c in range(tq // tk):
        qt = q_ref[c]
        zero = jnp.zeros_like(qt)
        qq_sc[:, c * tk:(c + 1) * tk] = jnp.where(row < DA_QK_DIM, qt, zero)
        qq_sc[:, tq + c * tk:tq + (c + 1) * tk] = jnp.where(row >= DA_QK_DIM, qt, zero)

    l_sc[...] = jnp.zeros(l_sc.shape, F32)
    acc_sc[...] = jnp.zeros(acc_sc.shape, F32)

    strips = [slice(c * STRIP, (c + 1) * STRIP) for c in range(2 * tq // STRIP)]

    def key_chunk(t):
        return k_ref[pl.ds(pl.multiple_of(t * tk, tk), tk), :]

    def causal(diag, sl):
        q_lo = sl.start % tq
        if q_lo >= (diag + 1) * tk - 1:
            return None
        kpos = diag * tk + lax.broadcasted_iota(jnp.int32, (tk, STRIP), 0)
        qpos = q_lo + lax.broadcasted_iota(jnp.int32, (tk, STRIP), 1)
        return kpos <= qpos

    def visible(diag, sl):
        return (sl.start % tq) + STRIP - 1 >= diag * tk

    def probs_stages(t, par, diag=None):
        kj = key_chunk(t)
        for sl in strips:
            if diag is not None and not visible(diag, sl):
                continue
            p = jnp.exp2(jnp.dot(kj, qq_sc[:, sl], preferred_element_type=F32))
            mask = None if diag is None else causal(diag, sl)
            if mask is not None:
                p = jnp.where(mask, p, 0.0)
            l_sc[:, sl] += jnp.sum(p, axis=0, keepdims=True)
            p_sc[par][:, sl] = p.astype(BF16)
            yield

    def values_stages(t, par, diag=None):
        vt = vt_ref[t]
        for sl in strips:
            if diag is not None and not visible(diag, sl):
                continue
            acc_sc[:, sl] += jnp.dot(vt, p_sc[par][:, sl], preferred_element_type=F32)
            yield

    def probs(*args, **kwargs):
        _interleave(probs_stages(*args, **kwargs))

    def values_plain(*args, **kwargs):
        _interleave(values_stages(*args, **kwargs))

    def chain(*gens):
        for g in gens:
            yield from g

    def finalize_stages():
        lv = lam_ref[...]
        lam = (jnp.exp(jnp.sum(lv[0:1] * lv[1:2], axis=-1, keepdims=True))
               - jnp.exp(jnp.sum(lv[2:3] * lv[3:4], axis=-1, keepdims=True)) + LAMBDA_INIT)
        for c in range(tq // STRIP):
            m1 = slice(c * STRIP, (c + 1) * STRIP)
            m2 = slice(tq + c * STRIP, tq + (c + 1) * STRIP)
            o = acc_sc[:, m1] / l_sc[:, m1] - lam * (acc_sc[:, m2] / l_sc[:, m2])
            ms = jnp.mean(o * o, axis=0, keepdims=True)
            y = o * lax.rsqrt(ms + NORM_EPS) * g_ref[...] * (1.0 - LAMBDA_INIT)
            o_ref[m1, :] = y.T.astype(o_ref.dtype)
            yield

    def bounded_path():
        def pair(t):
            probs(t, 0)
            probs(t + 1, 1)
            values_plain(t, 0)
            values_plain(t + 1, 1)

        def four_pairs(i, carry):
            for j in range(4):
                pair(8 * i + 2 * j)
            return carry

        lax.fori_loop(0, lax.shift_right_logical(qi, 2), four_pairs, 0)

        @pl.when((qi & 2) == 2)
        def _():
            t = 2 * (qi & ~3)
            pair(t)
            pair(t + 2)

        @pl.when((qi & 1) == 1)
        def _():
            pair(2 * qi - 2)

        d0 = 2 * qi
        _interleave(chain(probs_stages(d0, 0, diag=0), probs_stages(d0 + 1, 1, diag=1),
                          values_stages(d0, 0, diag=0), values_stages(d0 + 1, 1, diag=1),
                          finalize_stages()),
                    mlstm_stages())

    def scores(t, par):
        kj = key_chunk(t)
        for sl in strips:
            s = jnp.dot(kj, qq_sc[:, sl], preferred_element_type=F32)
            s_sc[par][:, sl] = s
            c_sc[par][:, sl] = jnp.max(s, axis=0, keepdims=True)

    def softmax(par, diag):
        for sl in strips:
            s = s_sc[par][:, sl]
            mask = None if diag is None else causal(diag, sl)
            if mask is None:
                cmax = c_sc[par][:, sl]
            else:
                s = jnp.where(mask, s, NEG_BIG)
                cmax = jnp.max(s, axis=0, keepdims=True)
            m_old = m_sc[:, sl]
            m_new = jnp.maximum(m_old, cmax)
            p = jnp.exp2(s - m_new)
            alpha = jnp.exp2(m_old - m_new)
            l_sc[:, sl] = alpha * l_sc[:, sl] + jnp.sum(p, axis=0, keepdims=True)
            m_sc[:, sl] = m_new
            a_sc[par][:, sl] = alpha
            p_sc[par][:, sl] = p.astype(BF16)

    def values(t, par):
        vt = vt_ref[t]
        for sl in strips:
            acc_sc[:, sl] = a_sc[par][:, sl] * acc_sc[:, sl] + jnp.dot(
                vt, p_sc[par][:, sl], preferred_element_type=F32)

    def online_path():
        m_sc[...] = jnp.full(m_sc.shape, NEG_BIG, F32)
        p1_sc[...] = jnp.zeros(p1_sc.shape, BF16)
        a1_sc[...] = jnp.ones(a1_sc.shape, F32)
        scores(0, 0)
        _interleave(mlstm_stages())

        def pair(i, carry):
            t = 2 * i
            scores(t + 1, 1)
            softmax(0, None)
            values(jnp.maximum(t - 1, 0), 1)
            scores(t + 2, 0)
            softmax(1, None)
            values(t, 0)
            return carry

        lax.fori_loop(0, qi, pair, 0)
        t = 2 * qi
        scores(t + 1, 1)
        softmax(0, 0)
        values(jnp.maximum(t - 1, 0), 1)
        softmax(1, 1)
        values(t, 0)
        values(t + 1, 1)
        _interleave(finalize_stages())

    pl.when(bounded_ref[0] == 1)(bounded_path)
    pl.when(bounded_ref[0] != 1)(online_path)


def _attn_mlstm(bounded, lam4, qt, kn, vt, g_col, proj3, gates3, gates_t, conv_w, conv_b, bias_row,
                bias_col, out_g, tq, tk):
    B, S, _ = kn.shape
    nk, nq = S // tk, S // tq
    L = tq // DA_HEADS
    assert tq == 2 * tk, "a query block spans exactly two key chunks"
    assert L % 8 == 0 and L * DA_HEADS * nq == S
    full = lambda shape: pl.BlockSpec(shape, lambda b, h, i, f: (0,) * len(shape))
    chunk = lambda width, col: pl.BlockSpec((None, L, width), lambda b, h, i, f: (b, h * nq + i, col))
    nxt = lambda h, i: jnp.minimum(h * nq + i + 1, DA_HEADS * nq - 1)
    grid_spec = pltpu.PrefetchScalarGridSpec(
        num_scalar_prefetch=1,
        grid=(B, DA_HEADS, nq),
        in_specs=[
            full((4, DA_QK_DIM)),
            pl.BlockSpec((None, None, tq // tk, DA_HEAD_DIM, tk), lambda b, h, i, f: (b, h, i, 0, 0)),
            pl.BlockSpec((None, S, DA_HEAD_DIM), lambda b, h, i, f: (b, 0, h)),
            pl.BlockSpec((None, None, nk, DA_HEAD_DIM, tk), lambda b, h, i, f: (b, h, 0, 0, 0)),
            full((DA_HEAD_DIM, 1)),
            chunk(ML_QK_WIDTH, COL_ML_Q),
            chunk(ML_QK_WIDTH, COL_ML_K),
            chunk(ML_WIDTH, COL_ML_V),
            chunk(ML_WIDTH, COL_ML_O),
            chunk(LANES, 0),
            pl.BlockSpec((None, N_GATES, L), lambda b, h, i, f: (b, 0, h * nq + i)),
            pl.BlockSpec((None, L, LANES), lambda b, h, i, f: (b, nxt(h, i), 0)),
            pl.BlockSpec((None, N_GATES, L), lambda b, h, i, f: (b, 0, nxt(h, i))),
            full((CONV_WIDTH, 2 * ML_QK_WIDTH)),
            full((1, 2 * ML_QK_WIDTH)),
            full((1, LANES)),
            full((N_GATES, 1)),
            full((ML_HEADS, ML_V_DIM)),
        ],
        out_specs=[
            pl.BlockSpec((None, tq, DA_HEAD_DIM), lambda b, h, i, f: (b, i, h)),
            chunk(ML_WIDTH, 0),
        ],
        scratch_shapes=[
            pltpu.VMEM((DA_HEAD_DIM, 2 * tq), BF16),
            pltpu.VMEM((tk, 2 * tq), F32),
            pltpu.VMEM((tk, 2 * tq), F32),
            pltpu.VMEM((tk, 2 * tq), BF16),
            pltpu.VMEM((tk, 2 * tq), BF16),
            pltpu.VMEM((1, 2 * tq), F32),
            pltpu.VMEM((1, 2 * tq), F32),
            pltpu.VMEM((1, 2 * tq), F32),
            pltpu.VMEM((1, 2 * tq), F32),
            pltpu.VMEM((1, 2 * tq), F32),
            pltpu.VMEM((1, 2 * tq), F32),
            pltpu.VMEM((DA_HEAD_DIM, 2 * tq), F32),
            pltpu.VMEM((ML_HEADS, ML_QK_DIM, ML_V_DIM), F32),
            pltpu.VMEM((ML_HEADS, 1, ML_QK_DIM), F32),
            pltpu.VMEM((ML_HEADS, 1, 1), F32),
            pltpu.VMEM((16, 2 * ML_QK_WIDTH), F32),
            pltpu.VMEM((CONV_WIDTH + 1, L, L), BF16),
            pltpu.VMEM((L, L), F32),
            pltpu.VMEM((2, L, LANES), F32),
            pltpu.VMEM((2, N_GATES, L), F32),
        ],
    )
    return pl.pallas_call(
        functools.partial(_attn_body, tq=tq, tk=tk),
        grid_spec=grid_spec,
        out_shape=[jax.ShapeDtypeStruct((B, S, DA_WIDTH), BF16),
                   jax.ShapeDtypeStruct((B, S, ML_WIDTH), BF16)],
        compiler_params=_cparams(("parallel", "arbitrary", "arbitrary")),
        name="diff_attn_mlstm",
    )(bounded, lam4, qt, kn, vt, g_col, proj3, proj3, proj3, proj3, gates3, gates_t, gates3, gates_t,
      conv_w, conv_b, bias_row, bias_col, out_g)


def _outproj_body(x_ref, d_ref, m_ref, wd_ref, wm_ref, o_ref):
    o_ref[...] = (x_ref[...]
                  + jnp.dot(d_ref[...], wd_ref[...], preferred_element_type=F32)
                  + jnp.dot(m_ref[...], wm_ref[...], preferred_element_type=F32))


def _outproj(x2, d2, m2, w_out, bm):
    T = x2.shape[0]
    return pl.pallas_call(
        _outproj_body,
        grid=(T // bm,),
        in_specs=[
            pl.BlockSpec((bm, D_MODEL), lambda m: (m, 0)),
            pl.BlockSpec((bm, DA_WIDTH), lambda m: (m, 0)),
            pl.BlockSpec((bm, ML_WIDTH), lambda m: (m, 0)),
            pl.BlockSpec((DA_WIDTH, D_MODEL), lambda m: (0, 0)),
            pl.BlockSpec((ML_WIDTH, D_MODEL), lambda m: (1, 0)),
        ],
        out_specs=pl.BlockSpec((bm, D_MODEL), lambda m: (m, 0)),
        out_shape=jax.ShapeDtypeStruct((T, D_MODEL), F32),
        compiler_params=_cparams(("parallel",)),
        name="outproj",
    )(x2, d2, m2, w_out, w_out)


def _mlp_body(x_ref, g_ref, wu_ref, wd_ref, o_ref, h_ref):
    @pl.when(pl.program_id(1) == 0)
    def _():
        x = x_ref[...]
        ms = jnp.mean(x * x, axis=-1, keepdims=True)
        h_ref[...] = (x * lax.rsqrt(ms + NORM_EPS) * g_ref[...]).astype(BF16)
        o_ref[...] = x

    u = jnp.dot(h_ref[...], wu_ref[...], preferred_element_type=F32)
    a = jnp.square(jnp.maximum(u, 0.0)).astype(BF16)
    o_ref[...] += jnp.dot(a, wd_ref[...], preferred_element_type=F32)


def _mlp(x1, g, w_up, w_down, bm, tf):
    T = x1.shape[0]
    return pl.pallas_call(
        _mlp_body,
        grid=(T // bm, D_FF // tf),
        in_specs=[
            pl.BlockSpec((bm, D_MODEL), lambda m, f: (m, 0)),
            pl.BlockSpec((1, D_MODEL), lambda m, f: (0, 0)),
            pl.BlockSpec((D_MODEL, tf), lambda m, f: (0, f)),
            pl.BlockSpec((tf, D_MODEL), lambda m, f: (f, 0)),
        ],
        out_specs=pl.BlockSpec((bm, D_MODEL), lambda m, f: (m, 0)),
        out_shape=jax.ShapeDtypeStruct((T, D_MODEL), F32),
        scratch_shapes=[pltpu.VMEM((bm, D_MODEL), BF16)],
        compiler_params=_cparams(("parallel", "arbitrary")),
        name="mlp",
    )(x1, g, w_up, w_down)


def _tiles(B, S):
    T = B * S
    return dict(
        bm_in=min(1024, T), bn_in=1024,
        tp=min(512, S // 2),
        tq=min(1024, S),
        bm_out=min(512, T),
        bm_mlp=min(512, T), tf=1024,
    )


def kernel(x, norm1_g, w_in, ml_conv_w, ml_conv_b, ml_b_i, ml_b_f, ml_out_g, da_q_norm_g, da_k_norm_g, da_lambda_q1, da_lambda_k1, da_lambda_q2, da_lambda_k2, da_out_g, w_out, norm2_g, w_up, w_down):
    B, S, D = x.shape
    assert D == D_MODEL and norm1_g.shape[0] == 1, "single-layer kernel"
    t = _tiles(B, S)
    T = B * S
    x2 = x.reshape(T, D)

    w_in0 = w_in[0]
    w_main = w_in0.astype(BF16)
    w_gate = jnp.pad(w_in0[:, D_MAIN:], ((0, 0), (0, LANES - N_GATES))).astype(BF16)
    reps = STRIP // DA_QK_DIM
    gq2 = jnp.tile(da_q_norm_g[0], reps).reshape(1, STRIP) * (DA_QK_DIM ** -0.5 * math.log2(math.e))
    gk2 = jnp.tile(da_k_norm_g[0], reps).reshape(1, STRIP)
    score_bound = DA_QK_DIM * jnp.max(jnp.abs(gq2)) * jnp.max(jnp.abs(gk2))
    bounded = (score_bound <= SCORE_BOUND).astype(jnp.int32).reshape(1)
    lam4 = jnp.stack([da_lambda_q1[0], da_lambda_k1[0], da_lambda_q2[0], da_lambda_k2[0]])
    bias8 = jnp.concatenate([ml_b_i[0], ml_b_f[0]])
    bias_row = jnp.pad(bias8, (0, LANES - N_GATES)).reshape(1, LANES)
    bias_col = bias8.reshape(N_GATES, 1)

    proj, gates = _inproj(x2, norm1_g, w_main, w_gate, t["bm_in"], t["bn_in"])
    proj3 = proj.reshape(B, S, D_MAIN)
    gates3 = gates.reshape(B, S, LANES)
    gates_t = jnp.transpose(gates3[:, :, :N_GATES], (0, 2, 1))

    qt, kn, vt = _prep(proj3, gq2, gk2, t["tp"])
    d_out, m_out = _attn_mlstm(bounded, lam4, qt, kn, vt, da_out_g[0].reshape(DA_HEAD_DIM, 1),
                               proj3, gates3, gates_t, ml_conv_w[0], ml_conv_b, bias_row, bias_col,
                               ml_out_g[0], t["tq"], t["tp"])

    x1 = _outproj(x2, d_out.reshape(T, DA_WIDTH), m_out.reshape(T, ML_WIDTH),
                  w_out[0].astype(BF16), t["bm_out"])
    y = _mlp(x1, norm2_g, w_up[0].astype(BF16), w_down[0].astype(BF16), t["bm_mlp"], t["tf"])
    return y.reshape(B, S, D)
```

```python
import functools
import math

import jax
import jax.numpy as jnp
from jax import lax
from jax.experimental import pallas as pl
from jax.experimental.pallas import tpu as pltpu

F32 = jnp.float32
BF16 = jnp.bfloat16

D_MODEL = 2048
DA_HEADS = 8
DA_HEAD_DIM = 128
DA_QK_DIM = 64
DA_WIDTH = DA_HEADS * DA_HEAD_DIM
ML_HEADS = 4
ML_V_DIM = 256
ML_QK_DIM = 128
ML_WIDTH = ML_HEADS * ML_V_DIM
ML_QK_WIDTH = ML_HEADS * ML_QK_DIM
CONV_WIDTH = 4
D_FF = 4 * D_MODEL
D_MAIN = 3 * DA_WIDTH + 2 * ML_QK_WIDTH + 2 * ML_WIDTH
N_GATES = 2 * ML_HEADS
NORM_EPS = 1e-6
LAMBDA_INIT = 0.8 - 0.6 * math.exp(-0.3 * 0)
LANES = 128
STRIP = 256
NEG_BIG = -1e30
SCORE_BOUND = 60.0
VMEM_LIMIT = 56 * 1024 * 1024

COL_DA_Q, COL_DA_K, COL_DA_V = 0, 1, 2
COL_ML_Q, COL_ML_K = 0, 1
COL_ML_V, COL_ML_O = 1, 2


def _cparams(sem):
    return pltpu.CompilerParams(dimension_semantics=sem, vmem_limit_bytes=VMEM_LIMIT)


def _inproj_body(x_ref, g_ref, w_ref, wg_ref, qk_ref, vt_ref, ml_ref, og_ref, h_ref, *, tk):
    n = pl.program_id(1)
    bm = h_ref.shape[0]

    @pl.when(n == 0)
    def _():
        x = x_ref[...]
        ms = jnp.mean(x * x, axis=-1, keepdims=True)
        hb = (x * lax.rsqrt(ms + NORM_EPS) * g_ref[...]).astype(BF16)
        h_ref[...] = hb
        og_ref[...] = jnp.dot(hb, wg_ref[...], preferred_element_type=F32)

    n_strips = w_ref.shape[1] // STRIP

    def strip(c):
        return jnp.dot(h_ref[...], w_ref[:, c * STRIP:(c + 1) * STRIP], preferred_element_type=F32)

    def put_transposed(dst_ref, y, c):
        for j in range(STRIP // DA_HEAD_DIM):
            h = c * (STRIP // DA_HEAD_DIM) + j
            for ck in range(bm // tk):
                dst_ref[h, ck] = y[ck * tk:(ck + 1) * tk,
                                   j * DA_HEAD_DIM:(j + 1) * DA_HEAD_DIM].T.astype(BF16)

    def plain(dst_ref):
        dst_ref[...] = jnp.dot(h_ref[...], w_ref[...], preferred_element_type=F32).astype(dst_ref.dtype)

    @pl.when(n < COL_DA_V)
    def _():
        plain(qk_ref)

    @pl.when(n == COL_DA_V)
    def _():
        for c in range(n_strips):
            put_transposed(vt_ref, strip(c), c)

    @pl.when(n > COL_DA_V)
    def _():
        plain(ml_ref)


def _inproj(x2, g, w_main, w_gate, B, S, bm, bn, tk):
    T = x2.shape[0]
    assert bn == DA_WIDTH and S % bm == 0 and bm % tk == 0
    mpb = S // bm
    n_ml = (D_MAIN - 3 * DA_WIDTH) // bn
    return pl.pallas_call(
        functools.partial(_inproj_body, tk=tk),
        grid=(T // bm, D_MAIN // bn),
        in_specs=[
            pl.BlockSpec((bm, D_MODEL), lambda m, n: (m, 0)),
            pl.BlockSpec((1, D_MODEL), lambda m, n: (0, 0)),
            pl.BlockSpec((D_MODEL, bn), lambda m, n: (0, n)),
            pl.BlockSpec((D_MODEL, LANES), lambda m, n: (0, 0)),
        ],
        out_specs=[
            pl.BlockSpec((bm, bn), lambda m, n: (m, jnp.minimum(n, COL_DA_K))),
            pl.BlockSpec((None, DA_HEADS, bm // tk, DA_HEAD_DIM, tk),
                         lambda m, n: (m // mpb, 0, m % mpb, 0, 0)),
            pl.BlockSpec((bm, bn), lambda m, n: (m, jnp.maximum(n - 3, 0))),
            pl.BlockSpec((bm, LANES), lambda m, n: (m, 0)),
        ],
        out_shape=[
            jax.ShapeDtypeStruct((T, 2 * DA_WIDTH), BF16),
            jax.ShapeDtypeStruct((B, DA_HEADS, S // tk, DA_HEAD_DIM, tk), BF16),
            jax.ShapeDtypeStruct((T, n_ml * bn), BF16),
            jax.ShapeDtypeStruct((T, LANES), F32),
        ],
        scratch_shapes=[pltpu.VMEM((bm, D_MODEL), BF16)],
        compiler_params=_cparams(("parallel", "arbitrary")),
        name="inproj",
    )(x2, g, w_main, w_gate)


def _prep_body(q_ref, k_ref, gq_ref, gk_ref, qt_ref, kn_ref):
    r = lax.broadcasted_iota(jnp.int32, (STRIP, STRIP), 0) // DA_QK_DIM
    c = lax.broadcasted_iota(jnp.int32, (STRIP, STRIP), 1) // DA_QK_DIM
    group = jnp.where(r == c, 1.0 / DA_QK_DIM, 0.0).astype(BF16)

    def norm(x, g):
        ms = jnp.dot((x * x).astype(BF16), group, preferred_element_type=F32)
        return x * lax.rsqrt(ms + NORM_EPS) * g

    for hp in range(DA_WIDTH // STRIP):
        sl = slice(hp * STRIP, (hp + 1) * STRIP)
        qn = norm(q_ref[:, sl].astype(F32), gq_ref[...])
        kn_ref[:, sl] = norm(k_ref[:, sl].astype(F32), gk_ref[...]).astype(BF16)
        for j in range(STRIP // DA_HEAD_DIM):
            hs = slice(j * DA_HEAD_DIM, (j + 1) * DA_HEAD_DIM)
            qt_ref[hp * (STRIP // DA_HEAD_DIM) + j] = qn[:, hs].T.astype(BF16)


def _prep(qk3, gq2, gk2, tp):
    B, S, _ = qk3.shape
    nk = S // tp
    blk = lambda col: pl.BlockSpec((None, tp, DA_WIDTH), lambda b, i, col=col: (b, i, col))
    vec = pl.BlockSpec((1, STRIP), lambda b, i: (0, 0))
    return pl.pallas_call(
        _prep_body,
        grid=(B, nk),
        in_specs=[blk(COL_DA_Q), blk(COL_DA_K), vec, vec],
        out_specs=[
            pl.BlockSpec((None, DA_HEADS, None, DA_HEAD_DIM, tp), lambda b, i: (b, 0, i, 0, 0)),
            pl.BlockSpec((None, tp, DA_WIDTH), lambda b, i: (b, i, 0)),
        ],
        out_shape=[
            jax.ShapeDtypeStruct((B, DA_HEADS, nk, DA_HEAD_DIM, tp), BF16),
            jax.ShapeDtypeStruct((B, S, DA_WIDTH), BF16),
        ],
        compiler_params=_cparams(("parallel", "parallel")),
        name="attn_prep",
    )(qk3, qk3, gq2, gk2)


def _log_sigmoid(x):
    return jnp.minimum(x, 0.0) - jnp.log1p(jnp.exp(-jnp.abs(x)))


def _split3(x):
    hi = x.astype(BF16)
    r1 = x - hi.astype(F32)
    mid = r1.astype(BF16)
    lo = (r1 - mid.astype(F32)).astype(BF16)
    return hi, mid, lo


def _mlstm_gates(gc_ref, gr_ref, brow_ref, bcol_ref, tri_sc, gcol_sc, grow_sc):
    gc = gc_ref[...] + brow_ref[...]
    gr = gr_ref[...] + bcol_ref[...]
    gcol_sc[0] = gc
    grow_sc[0] = gr
    gcol_sc[1] = sum(jnp.dot(tri_sc[0], part, preferred_element_type=F32)
                     for part in _split3(_log_sigmoid(gc)))
    grow_sc[1] = sum(jnp.dot(part, tri_sc[1], preferred_element_type=F32)
                     for part in _split3(_log_sigmoid(gr)))


def _mlstm_reset(c_sc, n_sc, m_sc, tail_sc, tri_sc, bias_sc, *, L):
    c_sc[...] = jnp.zeros(c_sc.shape, F32)
    n_sc[...] = jnp.zeros(n_sc.shape, F32)
    m_sc[...] = jnp.zeros(m_sc.shape, F32)
    tail_sc[...] = jnp.zeros(tail_sc.shape, F32)
    ti = lax.broadcasted_iota(jnp.int32, (L, L), 0)
    si = lax.broadcasted_iota(jnp.int32, (L, L), 1)
    tri_sc[0] = (si <= ti).astype(BF16)
    tri_sc[1] = (ti <= si).astype(BF16)
    for j in range(1, CONV_WIDTH):
        tri_sc[1 + j] = (si == ti - j).astype(BF16)
    bias_sc[...] = jnp.where(si <= ti, 0.0, NEG_BIG)


def _mlstm_stages(mq_ref, mk_ref, mv_ref, mo_ref, gc_ref, gr_ref, gcn_ref, grn_ref, cw_ref, cb_ref,
                  brow_ref, bcol_ref, og_ref, out_ref, c_sc, n_sc, m_sc, tail_sc, tri_sc, bias_sc,
                  gcol_sc, grow_sc, *, L):
    gc, b_cols = gcol_sc[0], gcol_sc[1]
    gr, b_rows = grow_sc[0], grow_sc[1]
    _mlstm_gates(gcn_ref, grn_ref, brow_ref, bcol_ref, tri_sc, gcol_sc, grow_sc)
    yield

    qk = []
    for src_ref in (mq_ref, mk_ref):
        for c in range(ML_QK_WIDTH // STRIP):
            sl = slice(c * STRIP, (c + 1) * STRIP)
            dst = slice(len(qk) * STRIP, (len(qk) + 1) * STRIP)
            xb = src_ref[:, sl]
            x = xb.astype(F32)
            y = cb_ref[:, dst] + cw_ref[CONV_WIDTH - 1:CONV_WIDTH, dst] * x
            fix = jnp.zeros((8, STRIP), F32)
            for j in range(1, CONV_WIDTH):
                w = cw_ref[CONV_WIDTH - 1 - j:CONV_WIDTH - j, dst]
                y = y + w * jnp.dot(tri_sc[1 + j], xb, preferred_element_type=F32)
                fix = fix + w * tail_sc[8 - j:16 - j, dst]
            y = jnp.concatenate([y[:8] + fix, y[8:]], axis=0)
            tail_sc[0:8, dst] = x[L - 8:L, :]
            half = 0.5 * y
            qk.append(half + half * jnp.tanh(half))
            yield
    per_strip = STRIP // ML_QK_DIM
    head_cols = lambda j: qk[j // per_strip][:, (j % per_strip) * ML_QK_DIM:(j % per_strip + 1) * ML_QK_DIM]

    heads = range(ML_HEADS)
    q = [head_cols(h) * (ML_QK_DIM ** -0.5) for h in heads]
    k = [head_cols(ML_HEADS + h) for h in heads]
    v = [mv_ref[:, h * ML_V_DIM:(h + 1) * ML_V_DIM] for h in heads]
    i_col = [gc[:, h:h + 1] for h in heads]
    i_row = [gr[h:h + 1, :] for h in heads]
    b_col = [b_cols[:, ML_HEADS + h:ML_HEADS + h + 1] for h in heads]
    b_row = [b_rows[ML_HEADS + h:ML_HEADS + h + 1, :] for h in heads]
    num, den, m_t = {}, {}, {}

    for h in heads:
        m_prev = m_sc[h]
        log_inter = b_col[h] + m_prev
        dmat = b_col[h] - b_row[h] + i_row[h] + bias_sc[...]
        m_t[h] = jnp.maximum(log_inter, jnp.max(dmat, axis=1, keepdims=True))
        inter_w = jnp.exp(log_inter - m_t[h])
        qb = q[h].astype(BF16)
        s_qk = lax.dot_general(qb, k[h].astype(BF16), (((1,), (1,)), ((), ())),
                               preferred_element_type=F32)
        p = jnp.exp(dmat - m_t[h]) * s_qk
        num[h] = (inter_w * jnp.dot(qb, c_sc[h].astype(BF16), preferred_element_type=F32)
                  + jnp.dot(p.astype(BF16), v[h], preferred_element_type=F32))
        den[h] = (inter_w * jnp.sum(q[h] * n_sc[h], axis=1, keepdims=True)
                  + jnp.sum(p, axis=1, keepdims=True))
        yield

    for h in heads:
        hh = num[h] * (1.0 / jnp.maximum(jnp.abs(den[h]), jnp.exp(-m_t[h])))
        ms = jnp.mean(hh * hh, axis=1, keepdims=True)
        yh = hh * lax.rsqrt(ms + NORM_EPS) * og_ref[h:h + 1, :]
        o_half = 0.5 * mo_ref[:, h * ML_V_DIM:(h + 1) * ML_V_DIM].astype(F32)
        gate = 0.5 + 0.5 * jnp.tanh(o_half)
        out_ref[:, h * ML_V_DIM:(h + 1) * ML_V_DIM] = (yh * gate).astype(out_ref.dtype)
        yield

    for h in heads:
        m_prev = m_sc[h]
        a = b_col[h][L - 1:L, :]
        g_col = a - b_col[h] + i_col[h]
        g_max = jnp.max(g_col, axis=0, keepdims=True)
        kw = k[h] * jnp.exp(g_col - g_max)
        c_loc = lax.dot_general(kw.astype(BF16), v[h], (((0,), (0,)), ((), ())),
                                preferred_element_type=F32)
        n_loc = jnp.sum(kw, axis=0, keepdims=True)
        m_new = jnp.maximum(a + m_prev, g_max)
        decay = jnp.exp(a + m_prev - m_new)
        scale = jnp.exp(g_max - m_new)
        c_sc[h] = decay * c_sc[h] + scale * c_loc
        n_sc[h] = decay * n_sc[h] + scale * n_loc
        m_sc[h] = m_new
        yield


def _interleave(*stage_generators):
    live = list(stage_generators)
    while live:
        live = [g for g in live if next(g, StopIteration) is not StopIteration]


N_ATTN_IN, N_MLSTM_IN, N_ATTN_SCRATCH = 5, 13, 12


def _attn_body(bounded_ref, *refs, tq, tk):
    lam_ref, q_ref, k_ref, vt_ref, g_ref = refs[:N_ATTN_IN]
    ml_in = refs[N_ATTN_IN:N_ATTN_IN + N_MLSTM_IN]
    o_ref, ml_out = refs[N_ATTN_IN + N_MLSTM_IN:N_ATTN_IN + N_MLSTM_IN + 2]
    scratch = refs[N_ATTN_IN + N_MLSTM_IN + 2:]
    (qq_sc, s0_sc, s1_sc, p0_sc, p1_sc, a0_sc, a1_sc, c0_sc, c1_sc, m_sc, l_sc,
     acc_sc) = scratch[:N_ATTN_SCRATCH]
    ml_state = scratch[N_ATTN_SCRATCH:]
    s_sc, p_sc, a_sc, c_sc = (s0_sc, s1_sc), (p0_sc, p1_sc), (a0_sc, a1_sc), (c0_sc, c1_sc)
    qi = pl.program_id(2)

    @pl.when((pl.program_id(1) == 0) & (qi == 0))
    def _():
        (_, _, _, _, gc_ref, gr_ref, _, _, _, _, brow_ref, bcol_ref, _) = ml_in
        (ml_c, ml_n, ml_m, tail_sc, tri_sc, bias_sc, gcol_sc, grow_sc) = ml_state
        _mlstm_reset(ml_c, ml_n, ml_m, tail_sc, tri_sc, bias_sc, L=tq // DA_HEADS)
        _mlstm_gates(gc_ref, gr_ref, brow_ref, bcol_ref, tri_sc, gcol_sc, grow_sc)

    def mlstm_stages():
        return _mlstm_stages(*ml_in, ml_out, *ml_state, L=tq // DA_HEADS)

    row = lax.broadcasted_iota(jnp.int32, (DA_HEAD_DIM, tk), 0)
    for c in range(tq // tk):
        qt = q_ref[c]
        zero = jnp.zeros_like(qt)
        qq_sc[:, c * tk:(c + 1) * tk] = jnp.where(row < DA_QK_DIM, qt, zero)
        qq_sc[:, tq + c * tk:tq + (c + 1) * tk] = jnp.where(row >= DA_QK_DIM, qt, zero)

    l_sc[...] = jnp.zeros(l_sc.shape, F32)
    acc_sc[...] = jnp.zeros(acc_sc.shape, F32)

    strips = [slice(c * STRIP, (c + 1) * STRIP) for c in range(2 * tq // STRIP)]

    def key_chunk(t):
        return k_ref[pl.ds(pl.multiple_of(t * tk, tk), tk), :]

    def causal(diag, sl):
        q_lo = sl.start % tq
        if q_lo >= (diag + 1) * tk - 1:
            return None
        kpos = diag * tk + lax.broadcasted_iota(jnp.int32, (tk, STRIP), 0)
        qpos = q_lo + lax.broadcasted_iota(jnp.int32, (tk, STRIP), 1)
        return kpos <= qpos

    def visible(diag, sl):
        return (sl.start % tq) + STRIP - 1 >= diag * tk

    def probs_stages(t, par, diag=None):
        kj = key_chunk(t)
        for sl in strips:
            if diag is not None and not visible(diag, sl):
                continue
            p = jnp.exp2(jnp.dot(kj, qq_sc[:, sl], preferred_element_type=F32))
            mask = None if diag is None else causal(diag, sl)
            if mask is not None:
                p = jnp.where(mask, p, 0.0)
            l_sc[:, sl] += jnp.sum(p, axis=0, keepdims=True)
            p_sc[par][:, sl] = p.astype(BF16)
            yield

    def values_stages(t, par, diag=None):
        vt = vt_ref[t]
        for sl in strips:
            if diag is not None and not visible(diag, sl):
                continue
            acc_sc[:, sl] += jnp.dot(vt, p_sc[par][:, sl], preferred_element_type=F32)
            yield

    def probs(*args, **kwargs):
        _interleave(probs_stages(*args, **kwargs))

    def values_plain(*args, **kwargs):
        _interleave(values_stages(*args, **kwargs))

    def chain(*gens):
        for g in gens:
            yield from g

    def finalize_stages():
        lv = lam_ref[...]
        lam = (jnp.exp(jnp.sum(lv[0:1] * lv[1:2], axis=-1, keepdims=True))
               - jnp.exp(jnp.sum(lv[2:3] * lv[3:4], axis=-1, keepdims=True)) + LAMBDA_INIT)
        for c in range(tq // STRIP):
            m1 = slice(c * STRIP, (c + 1) * STRIP)
            m2 = slice(tq + c * STRIP, tq + (c + 1) * STRIP)
            o = acc_sc[:, m1] / l_sc[:, m1] - lam * (acc_sc[:, m2] / l_sc[:, m2])
            ms = jnp.mean(o * o, axis=0, keepdims=True)
            y = o * lax.rsqrt(ms + NORM_EPS) * g_ref[...] * (1.0 - LAMBDA_INIT)
            o_ref[m1, :] = y.T.astype(o_ref.dtype)
            yield

    def bounded_path():
        def pair(t):
            probs(t, 0)
            probs(t + 1, 1)
            values_plain(t, 0)
            values_plain(t + 1, 1)

        def four_pairs(i, carry):
            for j in range(4):
                pair(8 * i + 2 * j)
            return carry

        lax.fori_loop(0, lax.shift_right_logical(qi, 2), four_pairs, 0)

        @pl.when((qi & 2) == 2)
        def _():
            t = 2 * (qi & ~3)
            pair(t)
            pair(t + 2)

        @pl.when((qi & 1) == 1)
        def _():
            pair(2 * qi - 2)

        d0 = 2 * qi
        _interleave(chain(probs_stages(d0, 0, diag=0), probs_stages(d0 + 1, 1, diag=1),
                          values_stages(d0, 0, diag=0), values_stages(d0 + 1, 1, diag=1),
                          finalize_stages()),
                    mlstm_stages())

    def scores(t, par):
        kj = key_chunk(t)
        for sl in strips:
            s = jnp.dot(kj, qq_sc[:, sl], preferred_element_type=F32)
            s_sc[par][:, sl] = s
            c_sc[par][:, sl] = jnp.max(s, axis=0, keepdims=True)

    def softmax(par, diag):
        for sl in strips:
            s = s_sc[par][:, sl]
            mask = None if diag is None else causal(diag, sl)
            if mask is None:
                cmax = c_sc[par][:, sl]
            else:
                s = jnp.where(mask, s, NEG_BIG)
                cmax = jnp.max(s, axis=0, keepdims=True)
            m_old = m_sc[:, sl]
            m_new = jnp.maximum(m_old, cmax)
            p = jnp.exp2(s - m_new)
            alpha = jnp.exp2(m_old - m_new)
            l_sc[:, sl] = alpha * l_sc[:, sl] + jnp.sum(p, axis=0, keepdims=True)
            m_sc[:, sl] = m_new
            a_sc[par][:, sl] = alpha
            p_sc[par][:, sl] = p.astype(BF16)

    def values(t, par):
        vt = vt_ref[t]
        for sl in strips:
            acc_sc[:, sl] = a_sc[par][:, sl] * acc_sc[:, sl] + jnp.dot(
                vt, p_sc[par][:, sl], preferred_element_type=F32)

    def online_path():
        m_sc[...] = jnp.full(m_sc.shape, NEG_BIG, F32)
        p1_sc[...] = jnp.zeros(p1_sc.shape, BF16)
        a1_sc[...] = jnp.ones(a1_sc.shape, F32)
        scores(0, 0)
        _interleave(mlstm_stages())

        def pair(i, carry):
            t = 2 * i
            scores(t + 1, 1)
            softmax(0, None)
            values(jnp.maximum(t - 1, 0), 1)
            scores(t + 2, 0)
            softmax(1, None)
            values(t, 0)
            return carry

        lax.fori_loop(0, qi, pair, 0)
        t = 2 * qi
        scores(t + 1, 1)
        softmax(0, 0)
        values(jnp.maximum(t - 1, 0), 1)
        softmax(1, 1)
        values(t, 0)
        values(t + 1, 1)
        _interleave(finalize_stages())

    pl.when(bounded_ref[0] == 1)(bounded_path)
    pl.when(bounded_ref[0] != 1)(online_path)


def _attn_mlstm(bounded, lam4, qt, kn, vt, g_col, proj3, gates3, gates_t, conv_w, conv_b, bias_row,
                bias_col, out_g, tq, tk):
    B, S, _ = kn.shape
    nk, nq = S // tk, S // tq
    L = tq // DA_HEADS
    assert tq == 2 * tk, "a query block spans exactly two key chunks"
    assert L % 8 == 0 and L * DA_HEADS * nq == S
    full = lambda shape: pl.BlockSpec(shape, lambda b, h, i, f: (0,) * len(shape))
    chunk = lambda width, col: pl.BlockSpec((None, L, width), lambda b, h, i, f: (b, h * nq + i, col))
    nxt = lambda h, i: jnp.minimum(h * nq + i + 1, DA_HEADS * nq - 1)
    grid_spec = pltpu.PrefetchScalarGridSpec(
        num_scalar_prefetch=1,
        grid=(B, DA_HEADS, nq),
        in_specs=[
            full((4, DA_QK_DIM)),
            pl.BlockSpec((None, None, tq // tk, DA_HEAD_DIM, tk), lambda b, h, i, f: (b, h, i, 0, 0)),
            pl.BlockSpec((None, S, DA_HEAD_DIM), lambda b, h, i, f: (b, 0, h)),
            pl.BlockSpec((None, None, nk, DA_HEAD_DIM, tk), lambda b, h, i, f: (b, h, 0, 0, 0)),
            full((DA_HEAD_DIM, 1)),
            chunk(ML_QK_WIDTH, COL_ML_Q),
            chunk(ML_QK_WIDTH, COL_ML_K),
            chunk(ML_WIDTH, COL_ML_V),
            chunk(ML_WIDTH, COL_ML_O),
            chunk(LANES, 0),
            pl.BlockSpec((None, N_GATES, L), lambda b, h, i, f: (b, 0, h * nq + i)),
            pl.BlockSpec((None, L, LANES), lambda b, h, i, f: (b, nxt(h, i), 0)),
            pl.BlockSpec((None, N_GATES, L), lambda b, h, i, f: (b, 0, nxt(h, i))),
            full((CONV_WIDTH, 2 * ML_QK_WIDTH)),
            full((1, 2 * ML_QK_WIDTH)),
            full((1, LANES)),
            full((N_GATES, 1)),
            full((ML_HEADS, ML_V_DIM)),
        ],
        out_specs=[
            pl.BlockSpec((None, tq, DA_HEAD_DIM), lambda b, h, i, f: (b, i, h)),
            chunk(ML_WIDTH, 0),
        ],
        scratch_shapes=[
            pltpu.VMEM((DA_HEAD_DIM, 2 * tq), BF16),
            pltpu.VMEM((tk, 2 * tq), F32),
            pltpu.VMEM((tk, 2 * tq), F32),
            pltpu.VMEM((tk, 2 * tq), BF16),
            pltpu.VMEM((tk, 2 * tq), BF16),
            pltpu.VMEM((1, 2 * tq), F32),
            pltpu.VMEM((1, 2 * tq), F32),
            pltpu.VMEM((1, 2 * tq), F32),
            pltpu.VMEM((1, 2 * tq), F32),
            pltpu.VMEM((1, 2 * tq), F32),
            pltpu.VMEM((1, 2 * tq), F32),
            pltpu.VMEM((DA_HEAD_DIM, 2 * tq), F32),
            pltpu.VMEM((ML_HEADS, ML_QK_DIM, ML_V_DIM), F32),
            pltpu.VMEM((ML_HEADS, 1, ML_QK_DIM), F32),
            pltpu.VMEM((ML_HEADS, 1, 1), F32),
            pltpu.VMEM((16, 2 * ML_QK_WIDTH), F32),
            pltpu.VMEM((CONV_WIDTH + 1, L, L), BF16),
            pltpu.VMEM((L, L), F32),
            pltpu.VMEM((2, L, LANES), F32),
            pltpu.VMEM((2, N_GATES, L), F32),
        ],
    )
    return pl.pallas_call(
        functools.partial(_attn_body, tq=tq, tk=tk),
        grid_spec=grid_spec,
        out_shape=[jax.ShapeDtypeStruct((B, S, DA_WIDTH), BF16),
                   jax.ShapeDtypeStruct((B, S, ML_WIDTH), BF16)],
        compiler_params=_cparams(("parallel", "arbitrary", "arbitrary")),
        name="diff_attn_mlstm",
    )(bounded, lam4, qt, kn, vt, g_col, proj3, proj3, proj3, proj3, gates3, gates_t, gates3, gates_t,
      conv_w, conv_b, bias_row, bias_col, out_g)


def _outproj_body(x_ref, d_ref, m_ref, wd_ref, wm_ref, o_ref):
    o_ref[...] = (x_ref[...]
                  + jnp.dot(d_ref[...], wd_ref[...], preferred_element_type=F32)
                  + jnp.dot(m_ref[...], wm_ref[...], preferred_element_type=F32))


def _outproj(x2, d2, m2, w_out, bm):
    T = x2.shape[0]
    return pl.pallas_call(
        _outproj_body,
        grid=(T // bm,),
        in_specs=[
            pl.BlockSpec((bm, D_MODEL), lambda m: (m, 0)),
            pl.BlockSpec((bm, DA_WIDTH), lambda m: (m, 0)),
            pl.BlockSpec((bm, ML_WIDTH), lambda m: (m, 0)),
            pl.BlockSpec((DA_WIDTH, D_MODEL), lambda m: (0, 0)),
            pl.BlockSpec((ML_WIDTH, D_MODEL), lambda m: (1, 0)),
        ],
        out_specs=pl.BlockSpec((bm, D_MODEL), lambda m: (m, 0)),
        out_shape=jax.ShapeDtypeStruct((T, D_MODEL), F32),
        compiler_params=_cparams(("parallel",)),
        name="outproj",
    )(x2, d2, m2, w_out, w_out)


def _mlp_body(x_ref, g_ref, wu_ref, wd_ref, o_ref, h_ref):
    @pl.when(pl.program_id(1) == 0)
    def _():
        x = x_ref[...]
        ms = jnp.mean(x * x, axis=-1, keepdims=True)
        h_ref[...] = (x * lax.rsqrt(ms + NORM_EPS) * g_ref[...]).astype(BF16)
        o_ref[...] = x

    u = jnp.dot(h_ref[...], wu_ref[...], preferred_element_type=F32)
    a = jnp.square(jnp.maximum(u, 0.0)).astype(BF16)
    o_ref[...] += jnp.dot(a, wd_ref[...], preferred_element_type=F32)


def _mlp(x1, g, w_up, w_down, bm, tf):
    T = x1.shape[0]
    return pl.pallas_call(
        _mlp_body,
        grid=(T // bm, D_FF // tf),
        in_specs=[
            pl.BlockSpec((bm, D_MODEL), lambda m, f: (m, 0)),
            pl.BlockSpec((1, D_MODEL), lambda m, f: (0, 0)),
            pl.BlockSpec((D_MODEL, tf), lambda m, f: (0, f)),
            pl.BlockSpec((tf, D_MODEL), lambda m, f: (f, 0)),
        ],
        out_specs=pl.BlockSpec((bm, D_MODEL), lambda m, f: (m, 0)),
        out_shape=jax.ShapeDtypeStruct((T, D_MODEL), F32),
        scratch_shapes=[pltpu.VMEM((bm, D_MODEL), BF16)],
        compiler_params=_cparams(("parallel", "arbitrary")),
        name="mlp",
    )(x1, g, w_up, w_down)


def _tiles(B, S):
    T = B * S
    return dict(
        bm_in=min(1024, S), bn_in=DA_WIDTH,
        tp=min(512, S // 2),
        tq=min(1024, S),
        bm_out=min(512, T),
        bm_mlp=min(512, T), tf=1024,
    )


def kernel(x, norm1_g, w_in, ml_conv_w, ml_conv_b, ml_b_i, ml_b_f, ml_out_g, da_q_norm_g, da_k_norm_g, da_lambda_q1, da_lambda_k1, da_lambda_q2, da_lambda_k2, da_out_g, w_out, norm2_g, w_up, w_down):
    B, S, D = x.shape
    assert D == D_MODEL and norm1_g.shape[0] == 1, "single-layer kernel"
    t = _tiles(B, S)
    T = B * S
    x2 = x.reshape(T, D)

    w_in0 = w_in[0]
    w_main = w_in0.astype(BF16)
    w_gate = jnp.pad(w_in0[:, D_MAIN:], ((0, 0), (0, LANES - N_GATES))).astype(BF16)
    reps = STRIP // DA_QK_DIM
    gq2 = jnp.tile(da_q_norm_g[0], reps).reshape(1, STRIP) * (DA_QK_DIM ** -0.5 * math.log2(math.e))
    gk2 = jnp.tile(da_k_norm_g[0], reps).reshape(1, STRIP)
    score_bound = DA_QK_DIM * jnp.max(jnp.abs(gq2)) * jnp.max(jnp.abs(gk2))
    bounded = (score_bound <= SCORE_BOUND).astype(jnp.int32).reshape(1)
    lam4 = jnp.stack([da_lambda_q1[0], da_lambda_k1[0], da_lambda_q2[0], da_lambda_k2[0]])
    bias8 = jnp.concatenate([ml_b_i[0], ml_b_f[0]])
    bias_row = jnp.pad(bias8, (0, LANES - N_GATES)).reshape(1, LANES)
    bias_col = bias8.reshape(N_GATES, 1)

    qk, vt, ml, gates = _inproj(x2, norm1_g, w_main, w_gate, B, S, t["bm_in"], t["bn_in"], t["tp"])
    qt, kn = _prep(qk.reshape(B, S, 2 * DA_WIDTH), gq2, gk2, t["tp"])
    proj3 = ml.reshape(B, S, ml.shape[-1])
    gates3 = gates.reshape(B, S, LANES)
    gates_t = jnp.transpose(gates3[:, :, :N_GATES], (0, 2, 1))

    d_out, m_out = _attn_mlstm(bounded, lam4, qt, kn, vt, da_out_g[0].reshape(DA_HEAD_DIM, 1),
                               proj3, gates3, gates_t, ml_conv_w[0], ml_conv_b, bias_row, bias_col,
                               ml_out_g[0], t["tq"], t["tp"])

    x1 = _outproj(x2, d_out.reshape(T, DA_WIDTH), m_out.reshape(T, ML_WIDTH),
                  w_out[0].astype(BF16), t["bm_out"])
    y = _mlp(x1, norm2_g, w_up[0].astype(BF16), w_down[0].astype(BF16), t["bm_mlp"], t["tf"])
    return y.reshape(B, S, D)
```

```python
import functools
import math

import jax
import jax.numpy as jnp
from jax import lax
from jax.experimental import pallas as pl
from jax.experimental.pallas import tpu as pltpu

F32 = jnp.float32
BF16 = jnp.bfloat16

D_MODEL = 2048
DA_HEADS = 8
DA_HEAD_DIM = 128
DA_QK_DIM = 64
DA_WIDTH = DA_HEADS * DA_HEAD_DIM
ML_HEADS = 4
ML_V_DIM = 256
ML_QK_DIM = 128
ML_WIDTH = ML_HEADS * ML_V_DIM
ML_QK_WIDTH = ML_HEADS * ML_QK_DIM
CONV_WIDTH = 4
D_FF = 4 * D_MODEL
D_MAIN = 3 * DA_WIDTH + 2 * ML_QK_WIDTH + 2 * ML_WIDTH
N_GATES = 2 * ML_HEADS
NORM_EPS = 1e-6
LAMBDA_INIT = 0.8 - 0.6 * math.exp(-0.3 * 0)
LANES = 128
STRIP = 256
NEG_BIG = -1e30
SCORE_BOUND = 60.0
VMEM_LIMIT = 56 * 1024 * 1024

COL_DA_Q, COL_DA_K, COL_DA_V = 0, 1, 2
COL_ML_Q, COL_ML_K = 0, 1
COL_ML_V, COL_ML_O = 1, 2


def _cparams(sem):
    return pltpu.CompilerParams(dimension_semantics=sem, vmem_limit_bytes=VMEM_LIMIT)


def _inproj_body(x_ref, g_ref, w_ref, wg_ref, qk_ref, vt_ref, ml_ref, og_ref, ogt_ref, h_ref, *, tk):
    n = pl.program_id(1)
    bm = h_ref.shape[0]

    @pl.when(n == 0)
    def _():
        x = x_ref[...]
        ms = jnp.mean(x * x, axis=-1, keepdims=True)
        hb = (x * lax.rsqrt(ms + NORM_EPS) * g_ref[...]).astype(BF16)
        h_ref[...] = hb
        og = jnp.dot(hb, wg_ref[...], preferred_element_type=F32)
        og_ref[...] = og
        ogt_ref[...] = og.T[:N_GATES, :]

    n_strips = w_ref.shape[1] // STRIP

    def strip(c):
        return jnp.dot(h_ref[...], w_ref[:, c * STRIP:(c + 1) * STRIP], preferred_element_type=F32)

    def put_transposed(dst_ref, y, c):
        for j in range(STRIP // DA_HEAD_DIM):
            h = c * (STRIP // DA_HEAD_DIM) + j
            for ck in range(bm // tk):
                dst_ref[h, ck] = y[ck * tk:(ck + 1) * tk,
                                   j * DA_HEAD_DIM:(j + 1) * DA_HEAD_DIM].T.astype(BF16)

    def plain(dst_ref):
        dst_ref[...] = jnp.dot(h_ref[...], w_ref[...], preferred_element_type=F32).astype(dst_ref.dtype)

    @pl.when(n < COL_DA_V)
    def _():
        plain(qk_ref)

    @pl.when(n == COL_DA_V)
    def _():
        for c in range(n_strips):
            put_transposed(vt_ref, strip(c), c)

    @pl.when(n > COL_DA_V)
    def _():
        plain(ml_ref)


def _inproj(x2, g, w_main, w_gate, B, S, bm, bn, tk):
    T = x2.shape[0]
    assert bn == DA_WIDTH and S % bm == 0 and bm % tk == 0
    mpb = S // bm
    n_ml = (D_MAIN - 3 * DA_WIDTH) // bn
    return pl.pallas_call(
        functools.partial(_inproj_body, tk=tk),
        grid=(T // bm, D_MAIN // bn),
        in_specs=[
            pl.BlockSpec((bm, D_MODEL), lambda m, n: (m, 0)),
            pl.BlockSpec((1, D_MODEL), lambda m, n: (0, 0)),
            pl.BlockSpec((D_MODEL, bn), lambda m, n: (0, n)),
            pl.BlockSpec((D_MODEL, LANES), lambda m, n: (0, 0)),
        ],
        out_specs=[
            pl.BlockSpec((bm, bn), lambda m, n: (m, jnp.minimum(n, COL_DA_K))),
            pl.BlockSpec((None, DA_HEADS, bm // tk, DA_HEAD_DIM, tk),
                         lambda m, n: (m // mpb, 0, m % mpb, 0, 0)),
            pl.BlockSpec((bm, bn), lambda m, n: (m, jnp.maximum(n - 3, 0))),
            pl.BlockSpec((bm, LANES), lambda m, n: (m, 0)),
            pl.BlockSpec((None, N_GATES, bm), lambda m, n: (m // mpb, 0, m % mpb)),
        ],
        out_shape=[
            jax.ShapeDtypeStruct((T, 2 * DA_WIDTH), BF16),
            jax.ShapeDtypeStruct((B, DA_HEADS, S // tk, DA_HEAD_DIM, tk), BF16),
            jax.ShapeDtypeStruct((T, n_ml * bn), BF16),
            jax.ShapeDtypeStruct((T, LANES), F32),
            jax.ShapeDtypeStruct((B, N_GATES, S), F32),
        ],
        scratch_shapes=[pltpu.VMEM((bm, D_MODEL), BF16)],
        compiler_params=_cparams(("parallel", "arbitrary")),
        name="inproj",
    )(x2, g, w_main, w_gate)


def _prep_body(q_ref, k_ref, gq_ref, gk_ref, qt_ref, kn_ref):
    r = lax.broadcasted_iota(jnp.int32, (STRIP, STRIP), 0) // DA_QK_DIM
    c = lax.broadcasted_iota(jnp.int32, (STRIP, STRIP), 1) // DA_QK_DIM
    group = jnp.where(r == c, 1.0 / DA_QK_DIM, 0.0).astype(BF16)

    def norm(x, g):
        ms = jnp.dot((x * x).astype(BF16), group, preferred_element_type=F32)
        return x * lax.rsqrt(ms + NORM_EPS) * g

    for hp in range(DA_WIDTH // STRIP):
        sl = slice(hp * STRIP, (hp + 1) * STRIP)
        qn = norm(q_ref[:, sl].astype(F32), gq_ref[...])
        kn_ref[:, sl] = norm(k_ref[:, sl].astype(F32), gk_ref[...]).astype(BF16)
        for j in range(STRIP // DA_HEAD_DIM):
            hs = slice(j * DA_HEAD_DIM, (j + 1) * DA_HEAD_DIM)
            qt_ref[hp * (STRIP // DA_HEAD_DIM) + j] = qn[:, hs].T.astype(BF16)


def _prep(qk3, gq2, gk2, tp):
    B, S, _ = qk3.shape
    nk = S // tp
    blk = lambda col: pl.BlockSpec((None, tp, DA_WIDTH), lambda b, i, col=col: (b, i, col))
    vec = pl.BlockSpec((1, STRIP), lambda b, i: (0, 0))
    return pl.pallas_call(
        _prep_body,
        grid=(B, nk),
        in_specs=[blk(COL_DA_Q), blk(COL_DA_K), vec, vec],
        out_specs=[
            pl.BlockSpec((None, DA_HEADS, None, DA_HEAD_DIM, tp), lambda b, i: (b, 0, i, 0, 0)),
            pl.BlockSpec((None, tp, DA_WIDTH), lambda b, i: (b, i, 0)),
        ],
        out_shape=[
            jax.ShapeDtypeStruct((B, DA_HEADS, nk, DA_HEAD_DIM, tp), BF16),
            jax.ShapeDtypeStruct((B, S, DA_WIDTH), BF16),
        ],
        compiler_params=_cparams(("parallel", "parallel")),
        name="attn_prep",
    )(qk3, qk3, gq2, gk2)


def _log_sigmoid(x):
    return jnp.minimum(x, 0.0) - jnp.log1p(jnp.exp(-jnp.abs(x)))


def _split3(x):
    hi = x.astype(BF16)
    r1 = x - hi.astype(F32)
    mid = r1.astype(BF16)
    lo = (r1 - mid.astype(F32)).astype(BF16)
    return hi, mid, lo


def _mlstm_gates(gc_ref, gr_ref, brow_ref, bcol_ref, tri_sc, gcol_sc, grow_sc):
    gc = gc_ref[...] + brow_ref[...]
    gr = gr_ref[...] + bcol_ref[...]
    gcol_sc[0] = gc
    grow_sc[0] = gr
    gcol_sc[1] = sum(jnp.dot(tri_sc[0], part, preferred_element_type=F32)
                     for part in _split3(_log_sigmoid(gc)))
    grow_sc[1] = sum(jnp.dot(part, tri_sc[1], preferred_element_type=F32)
                     for part in _split3(_log_sigmoid(gr)))


def _mlstm_reset(c_sc, n_sc, m_sc, tail_sc, tri_sc, bias_sc, *, L):
    c_sc[...] = jnp.zeros(c_sc.shape, F32)
    n_sc[...] = jnp.zeros(n_sc.shape, F32)
    m_sc[...] = jnp.zeros(m_sc.shape, F32)
    tail_sc[...] = jnp.zeros(tail_sc.shape, F32)
    ti = lax.broadcasted_iota(jnp.int32, (L, L), 0)
    si = lax.broadcasted_iota(jnp.int32, (L, L), 1)
    tri_sc[0] = (si <= ti).astype(BF16)
    tri_sc[1] = (ti <= si).astype(BF16)
    for j in range(1, CONV_WIDTH):
        tri_sc[1 + j] = (si == ti - j).astype(BF16)
    bias_sc[...] = jnp.where(si <= ti, 0.0, NEG_BIG)


def _mlstm_stages(mq_ref, mk_ref, mv_ref, mo_ref, gc_ref, gr_ref, gcn_ref, grn_ref, cw_ref, cb_ref,
                  brow_ref, bcol_ref, og_ref, out_ref, c_sc, n_sc, m_sc, tail_sc, tri_sc, bias_sc,
                  gcol_sc, grow_sc, *, L):
    gc, b_cols = gcol_sc[0], gcol_sc[1]
    gr, b_rows = grow_sc[0], grow_sc[1]
    _mlstm_gates(gcn_ref, grn_ref, brow_ref, bcol_ref, tri_sc, gcol_sc, grow_sc)
    yield

    qk = []
    for src_ref in (mq_ref, mk_ref):
        for c in range(ML_QK_WIDTH // STRIP):
            sl = slice(c * STRIP, (c + 1) * STRIP)
            dst = slice(len(qk) * STRIP, (len(qk) + 1) * STRIP)
            xb = src_ref[:, sl]
            x = xb.astype(F32)
            y = cb_ref[:, dst] + cw_ref[CONV_WIDTH - 1:CONV_WIDTH, dst] * x
            fix = jnp.zeros((8, STRIP), F32)
            for j in range(1, CONV_WIDTH):
                w = cw_ref[CONV_WIDTH - 1 - j:CONV_WIDTH - j, dst]
                y = y + w * jnp.dot(tri_sc[1 + j], xb, preferred_element_type=F32)
                fix = fix + w * tail_sc[8 - j:16 - j, dst]
            y = jnp.concatenate([y[:8] + fix, y[8:]], axis=0)
            tail_sc[0:8, dst] = x[L - 8:L, :]
            half = 0.5 * y
            qk.append(half + half * jnp.tanh(half))
            yield
    per_strip = STRIP // ML_QK_DIM
    head_cols = lambda j: qk[j // per_strip][:, (j % per_strip) * ML_QK_DIM:(j % per_strip + 1) * ML_QK_DIM]

    heads = range(ML_HEADS)
    q = [head_cols(h) * (ML_QK_DIM ** -0.5) for h in heads]
    k = [head_cols(ML_HEADS + h) for h in heads]
    v = [mv_ref[:, h * ML_V_DIM:(h + 1) * ML_V_DIM] for h in heads]
    i_col = [gc[:, h:h + 1] for h in heads]
    i_row = [gr[h:h + 1, :] for h in heads]
    b_col = [b_cols[:, ML_HEADS + h:ML_HEADS + h + 1] for h in heads]
    b_row = [b_rows[ML_HEADS + h:ML_HEADS + h + 1, :] for h in heads]
    num, den, m_t = {}, {}, {}

    for h in heads:
        m_prev = m_sc[h]
        log_inter = b_col[h] + m_prev
        dmat = b_col[h] - b_row[h] + i_row[h] + bias_sc[...]
        m_t[h] = jnp.maximum(log_inter, jnp.max(dmat, axis=1, keepdims=True))
        inter_w = jnp.exp(log_inter - m_t[h])
        qb = q[h].astype(BF16)
        s_qk = lax.dot_general(qb, k[h].astype(BF16), (((1,), (1,)), ((), ())),
                               preferred_element_type=F32)
        p = jnp.exp(dmat - m_t[h]) * s_qk
        num[h] = (inter_w * jnp.dot(qb, c_sc[h].astype(BF16), preferred_element_type=F32)
                  + jnp.dot(p.astype(BF16), v[h], preferred_element_type=F32))
        den[h] = (inter_w * jnp.sum(q[h] * n_sc[h], axis=1, keepdims=True)
                  + jnp.sum(p, axis=1, keepdims=True))
        yield

    for h in heads:
        hh = num[h] * (1.0 / jnp.maximum(jnp.abs(den[h]), jnp.exp(-m_t[h])))
        ms = jnp.mean(hh * hh, axis=1, keepdims=True)
        yh = hh * lax.rsqrt(ms + NORM_EPS) * og_ref[h:h + 1, :]
        o_half = 0.5 * mo_ref[:, h * ML_V_DIM:(h + 1) * ML_V_DIM].astype(F32)
        gate = 0.5 + 0.5 * jnp.tanh(o_half)
        out_ref[:, h * ML_V_DIM:(h + 1) * ML_V_DIM] = (yh * gate).astype(out_ref.dtype)
        yield

    for h in heads:
        m_prev = m_sc[h]
        a = b_col[h][L - 1:L, :]
        g_col = a - b_col[h] + i_col[h]
        g_max = jnp.max(g_col, axis=0, keepdims=True)
        kw = k[h] * jnp.exp(g_col - g_max)
        c_loc = lax.dot_general(kw.astype(BF16), v[h], (((0,), (0,)), ((), ())),
                                preferred_element_type=F32)
        n_loc = jnp.sum(kw, axis=0, keepdims=True)
        m_new = jnp.maximum(a + m_prev, g_max)
        decay = jnp.exp(a + m_prev - m_new)
        scale = jnp.exp(g_max - m_new)
        c_sc[h] = decay * c_sc[h] + scale * c_loc
        n_sc[h] = decay * n_sc[h] + scale * n_loc
        m_sc[h] = m_new
        yield


def _interleave(*stage_generators):
    live = list(stage_generators)
    while live:
        live = [g for g in live if next(g, StopIteration) is not StopIteration]


N_ATTN_IN, N_MLSTM_IN, N_ATTN_SCRATCH = 5, 13, 12


def _attn_body(bounded_ref, *refs, tq, tk):
    lam_ref, q_ref, k_ref, vt_ref, g_ref = refs[:N_ATTN_IN]
    ml_in = refs[N_ATTN_IN:N_ATTN_IN + N_MLSTM_IN]
    o_ref, ml_out = refs[N_ATTN_IN + N_MLSTM_IN:N_ATTN_IN + N_MLSTM_IN + 2]
    scratch = refs[N_ATTN_IN + N_MLSTM_IN + 2:]
    (qq_sc, s0_sc, s1_sc, p0_sc, p1_sc, a0_sc, a1_sc, c0_sc, c1_sc, m_sc, l_sc,
     acc_sc) = scratch[:N_ATTN_SCRATCH]
    ml_state = scratch[N_ATTN_SCRATCH:]
    s_sc, p_sc, a_sc, c_sc = (s0_sc, s1_sc), (p0_sc, p1_sc), (a0_sc, a1_sc), (c0_sc, c1_sc)
    qi = pl.program_id(2)

    @pl.when((pl.program_id(1) == 0) & (qi == 0))
    def _():
        (_, _, _, _, gc_ref, gr_ref, _, _, _, _, brow_ref, bcol_ref, _) = ml_in
        (ml_c, ml_n, ml_m, tail_sc, tri_sc, bias_sc, gcol_sc, grow_sc) = ml_state
        _mlstm_reset(ml_c, ml_n, ml_m, tail_sc, tri_sc, bias_sc, L=tq // DA_HEADS)
        _mlstm_gates(gc_ref, gr_ref, brow_ref, bcol_ref, tri_sc, gcol_sc, grow_sc)

    def mlstm_stages():
        return _mlstm_stages(*ml_in, ml_out, *ml_state, L=tq // DA_HEADS)

    row = lax.broadcasted_iota(jnp.int32, (DA_HEAD_DIM, tk), 0)
    for c in range(tq // tk):
        qt = q_ref[c]
        zero = jnp.zeros_like(qt)
        qq_sc[:, c * tk:(c + 1) * tk] = jnp.where(row < DA_QK_DIM, qt, zero)
        qq_sc[:, tq + c * tk:tq + (c + 1) * tk] = jnp.where(row >= DA_QK_DIM, qt, zero)

    l_sc[...] = jnp.zeros(l_sc.shape, F32)
    acc_sc[...] = jnp.zeros(acc_sc.shape, F32)

    strips = [slice(c * STRIP, (c + 1) * STRIP) for c in range(2 * tq // STRIP)]

    def key_chunk(t):
        return k_ref[pl.ds(pl.multiple_of(t * tk, tk), tk), :]

    def causal(diag, sl):
        q_lo = sl.start % tq
        if q_lo >= (diag + 1) * tk - 1:
            return None
        kpos = diag * tk + lax.broadcasted_iota(jnp.int32, (tk, STRIP), 0)
        qpos = q_lo + lax.broadcasted_iota(jnp.int32, (tk, STRIP), 1)
        return kpos <= qpos

    def visible(diag, sl):
        return (sl.start % tq) + STRIP - 1 >= diag * tk

    def probs_stages(t, par, diag=None):
        kj = key_chunk(t)
        for sl in strips:
            if diag is not None and not visible(diag, sl):
                continue
            p = jnp.exp2(jnp.dot(kj, qq_sc[:, sl], preferred_element_type=F32))
            mask = None if diag is None else causal(diag, sl)
            if mask is not None:
                p = jnp.where(mask, p, 0.0)
            l_sc[:, sl] += jnp.sum(p, axis=0, keepdims=True)
            p_sc[par][:, sl] = p.astype(BF16)
            yield

    def values_stages(t, par, diag=None):
        vt = vt_ref[t]
        for sl in strips:
            if diag is not None and not visible(diag, sl):
                continue
            acc_sc[:, sl] += jnp.dot(vt, p_sc[par][:, sl], preferred_element_type=F32)
            yield

    def probs(*args, **kwargs):
        _interleave(probs_stages(*args, **kwargs))

    def values_plain(*args, **kwargs):
        _interleave(values_stages(*args, **kwargs))

    def chain(*gens):
        for g in gens:
            yield from g

    def finalize_stages():
        lv = lam_ref[...]
        lam = (jnp.exp(jnp.sum(lv[0:1] * lv[1:2], axis=-1, keepdims=True))
               - jnp.exp(jnp.sum(lv[2:3] * lv[3:4], axis=-1, keepdims=True)) + LAMBDA_INIT)
        for c in range(tq // STRIP):
            m1 = slice(c * STRIP, (c + 1) * STRIP)
            m2 = slice(tq + c * STRIP, tq + (c + 1) * STRIP)
            o = acc_sc[:, m1] / l_sc[:, m1] - lam * (acc_sc[:, m2] / l_sc[:, m2])
            ms = jnp.mean(o * o, axis=0, keepdims=True)
            y = o * lax.rsqrt(ms + NORM_EPS) * g_ref[...] * (1.0 - LAMBDA_INIT)
            o_ref[m1, :] = y.T.astype(o_ref.dtype)
            yield

    def bounded_path():
        def pair(t):
            probs(t, 0)
            probs(t + 1, 1)
            values_plain(t, 0)
            values_plain(t + 1, 1)

        def four_pairs(i, carry):
            for j in range(4):
                pair(8 * i + 2 * j)
            return carry

        lax.fori_loop(0, lax.shift_right_logical(qi, 2), four_pairs, 0)

        @pl.when((qi & 2) == 2)
        def _():
            t = 2 * (qi & ~3)
            pair(t)
            pair(t + 2)

        @pl.when((qi & 1) == 1)
        def _():
            pair(2 * qi - 2)

        d0 = 2 * qi
        _interleave(chain(probs_stages(d0, 0, diag=0), probs_stages(d0 + 1, 1, diag=1),
                          values_stages(d0, 0, diag=0), values_stages(d0 + 1, 1, diag=1),
                          finalize_stages()),
                    mlstm_stages())

    def scores(t, par):
        kj = key_chunk(t)
        for sl in strips:
            s = jnp.dot(kj, qq_sc[:, sl], preferred_element_type=F32)
            s_sc[par][:, sl] = s
            c_sc[par][:, sl] = jnp.max(s, axis=0, keepdims=True)

    def softmax(par, diag):
        for sl in strips:
            s = s_sc[par][:, sl]
            mask = None if diag is None else causal(diag, sl)
            if mask is None:
                cmax = c_sc[par][:, sl]
            else:
                s = jnp.where(mask, s, NEG_BIG)
                cmax = jnp.max(s, axis=0, keepdims=True)
            m_old = m_sc[:, sl]
            m_new = jnp.maximum(m_old, cmax)
            p = jnp.exp2(s - m_new)
            alpha = jnp.exp2(m_old - m_new)
            l_sc[:, sl] = alpha * l_sc[:, sl] + jnp.sum(p, axis=0, keepdims=True)
            m_sc[:, sl] = m_new
            a_sc[par][:, sl] = alpha
            p_sc[par][:, sl] = p.astype(BF16)

    def values(t, par):
        vt = vt_ref[t]
        for sl in strips:
            acc_sc[:, sl] = a_sc[par][:, sl] * acc_sc[:, sl] + jnp.dot(
                vt, p_sc[par][:, sl], preferred_element_type=F32)

    def online_path():
        m_sc[...] = jnp.full(m_sc.shape, NEG_BIG, F32)
        p1_sc[...] = jnp.zeros(p1_sc.shape, BF16)
        a1_sc[...] = jnp.ones(a1_sc.shape, F32)
        scores(0, 0)
        _interleave(mlstm_stages())

        def pair(i, carry):
            t = 2 * i
            scores(t + 1, 1)
            softmax(0, None)
            values(jnp.maximum(t - 1, 0), 1)
            scores(t + 2, 0)
            softmax(1, None)
            values(t, 0)
            return carry

        lax.fori_loop(0, qi, pair, 0)
        t = 2 * qi
        scores(t + 1, 1)
        softmax(0, 0)
        values(jnp.maximum(t - 1, 0), 1)
        softmax(1, 1)
        values(t, 0)
        values(t + 1, 1)
        _interleave(finalize_stages())

    pl.when(bounded_ref[0] == 1)(bounded_path)
    pl.when(bounded_ref[0] != 1)(online_path)


def _attn_mlstm(bounded, lam4, qt, kn, vt, g_col, proj3, gates3, gates_t, conv_w, conv_b, bias_row,
                bias_col, out_g, tq, tk):
    B, S, _ = kn.shape
    nk, nq = S // tk, S // tq
    L = tq // DA_HEADS
    assert tq == 2 * tk, "a query block spans exactly two key chunks"
    assert L % 8 == 0 and L * DA_HEADS * nq == S
    full = lambda shape: pl.BlockSpec(shape, lambda b, h, i, f: (0,) * len(shape))
    chunk = lambda width, col: pl.BlockSpec((None, L, width), lambda b, h, i, f: (b, h * nq + i, col))
    nxt = lambda h, i: jnp.minimum(h * nq + i + 1, DA_HEADS * nq - 1)
    grid_spec = pltpu.PrefetchScalarGridSpec(
        num_scalar_prefetch=1,
        grid=(B, DA_HEADS, nq),
        in_specs=[
            full((4, DA_QK_DIM)),
            pl.BlockSpec((None, None, tq // tk, DA_HEAD_DIM, tk), lambda b, h, i, f: (b, h, i, 0, 0)),
            pl.BlockSpec((None, S, DA_HEAD_DIM), lambda b, h, i, f: (b, 0, h)),
            pl.BlockSpec((None, None, nk, DA_HEAD_DIM, tk), lambda b, h, i, f: (b, h, 0, 0, 0)),
            full((DA_HEAD_DIM, 1)),
            chunk(ML_QK_WIDTH, COL_ML_Q),
            chunk(ML_QK_WIDTH, COL_ML_K),
            chunk(ML_WIDTH, COL_ML_V),
            chunk(ML_WIDTH, COL_ML_O),
            chunk(LANES, 0),
            pl.BlockSpec((None, N_GATES, L), lambda b, h, i, f: (b, 0, h * nq + i)),
            pl.BlockSpec((None, L, LANES), lambda b, h, i, f: (b, nxt(h, i), 0)),
            pl.BlockSpec((None, N_GATES, L), lambda b, h, i, f: (b, 0, nxt(h, i))),
            full((CONV_WIDTH, 2 * ML_QK_WIDTH)),
            full((1, 2 * ML_QK_WIDTH)),
            full((1, LANES)),
            full((N_GATES, 1)),
            full((ML_HEADS, ML_V_DIM)),
        ],
        out_specs=[
            pl.BlockSpec((None, tq, DA_HEAD_DIM), lambda b, h, i, f: (b, i, h)),
            chunk(ML_WIDTH, 0),
        ],
        scratch_shapes=[
            pltpu.VMEM((DA_HEAD_DIM, 2 * tq), BF16),
            pltpu.VMEM((tk, 2 * tq), F32),
            pltpu.VMEM((tk, 2 * tq), F32),
            pltpu.VMEM((tk, 2 * tq), BF16),
            pltpu.VMEM((tk, 2 * tq), BF16),
            pltpu.VMEM((1, 2 * tq), F32),
            pltpu.VMEM((1, 2 * tq), F32),
            pltpu.VMEM((1, 2 * tq), F32),
            pltpu.VMEM((1, 2 * tq), F32),
            pltpu.VMEM((1, 2 * tq), F32),
            pltpu.VMEM((1, 2 * tq), F32),
            pltpu.VMEM((DA_HEAD_DIM, 2 * tq), F32),
            pltpu.VMEM((ML_HEADS, ML_QK_DIM, ML_V_DIM), F32),
            pltpu.VMEM((ML_HEADS, 1, ML_QK_DIM), F32),
            pltpu.VMEM((ML_HEADS, 1, 1), F32),
            pltpu.VMEM((16, 2 * ML_QK_WIDTH), F32),
            pltpu.VMEM((CONV_WIDTH + 1, L, L), BF16),
            pltpu.VMEM((L, L), F32),
            pltpu.VMEM((2, L, LANES), F32),
            pltpu.VMEM((2, N_GATES, L), F32),
        ],
    )
    return pl.pallas_call(
        functools.partial(_attn_body, tq=tq, tk=tk),
        grid_spec=grid_spec,
        out_shape=[jax.ShapeDtypeStruct((B, S, DA_WIDTH), BF16),
                   jax.ShapeDtypeStruct((B, S, ML_WIDTH), BF16)],
        compiler_params=_cparams(("parallel", "arbitrary", "arbitrary")),
        name="diff_attn_mlstm",
    )(bounded, lam4, qt, kn, vt, g_col, proj3, proj3, proj3, proj3, gates3, gates_t, gates3, gates_t,
      conv_w, conv_b, bias_row, bias_col, out_g)


def _outproj_body(x_ref, d_ref, m_ref, wd_ref, wm_ref, o_ref):
    o_ref[...] = (x_ref[...]
                  + jnp.dot(d_ref[...], wd_ref[...], preferred_element_type=F32)
                  + jnp.dot(m_ref[...], wm_ref[...], preferred_element_type=F32))


def _outproj(x2, d2, m2, w_out, bm):
    T = x2.shape[0]
    return pl.pallas_call(
        _outproj_body,
        grid=(T // bm,),
        in_specs=[
            pl.BlockSpec((bm, D_MODEL), lambda m: (m, 0)),
            pl.BlockSpec((bm, DA_WIDTH), lambda m: (m, 0)),
            pl.BlockSpec((bm, ML_WIDTH), lambda m: (m, 0)),
            pl.BlockSpec((DA_WIDTH, D_MODEL), lambda m: (0, 0)),
            pl.BlockSpec((ML_WIDTH, D_MODEL), lambda m: (1, 0)),
        ],
        out_specs=pl.BlockSpec((bm, D_MODEL), lambda m: (m, 0)),
        out_shape=jax.ShapeDtypeStruct((T, D_MODEL), F32),
        compiler_params=_cparams(("parallel",)),
        name="outproj",
    )(x2, d2, m2, w_out, w_out)


def _mlp_body(x_ref, g_ref, wu_ref, wd_ref, o_ref, h_ref):
    @pl.when(pl.program_id(1) == 0)
    def _():
        x = x_ref[...]
        ms = jnp.mean(x * x, axis=-1, keepdims=True)
        h_ref[...] = (x * lax.rsqrt(ms + NORM_EPS) * g_ref[...]).astype(BF16)
        o_ref[...] = x

    u = jnp.dot(h_ref[...], wu_ref[...], preferred_element_type=F32)
    a = jnp.square(jnp.maximum(u, 0.0)).astype(BF16)
    o_ref[...] += jnp.dot(a, wd_ref[...], preferred_element_type=F32)


def _mlp(x1, g, w_up, w_down, bm, tf):
    T = x1.shape[0]
    return pl.pallas_call(
        _mlp_body,
        grid=(T // bm, D_FF // tf),
        in_specs=[
            pl.BlockSpec((bm, D_MODEL), lambda m, f: (m, 0)),
            pl.BlockSpec((1, D_MODEL), lambda m, f: (0, 0)),
            pl.BlockSpec((D_MODEL, tf), lambda m, f: (0, f)),
            pl.BlockSpec((tf, D_MODEL), lambda m, f: (f, 0)),
        ],
        out_specs=pl.BlockSpec((bm, D_MODEL), lambda m, f: (m, 0)),
        out_shape=jax.ShapeDtypeStruct((T, D_MODEL), F32),
        scratch_shapes=[pltpu.VMEM((bm, D_MODEL), BF16)],
        compiler_params=_cparams(("parallel", "arbitrary")),
        name="mlp",
    )(x1, g, w_up, w_down)


def _tiles(B, S):
    T = B * S
    return dict(
        bm_in=min(1024, S), bn_in=DA_WIDTH,
        tp=min(512, S // 2),
        tq=min(1024, S),
        bm_out=min(512, T),
        bm_mlp=min(512, T), tf=1024,
    )


def kernel(x, norm1_g, w_in, ml_conv_w, ml_conv_b, ml_b_i, ml_b_f, ml_out_g, da_q_norm_g, da_k_norm_g, da_lambda_q1, da_lambda_k1, da_lambda_q2, da_lambda_k2, da_out_g, w_out, norm2_g, w_up, w_down):
    B, S, D = x.shape
    assert D == D_MODEL and norm1_g.shape[0] == 1, "single-layer kernel"
    t = _tiles(B, S)
    T = B * S
    x2 = x.reshape(T, D)

    w_in0 = w_in[0]
    w_main = w_in0.astype(BF16)
    w_gate = jnp.pad(w_in0[:, D_MAIN:], ((0, 0), (0, LANES - N_GATES))).astype(BF16)
    reps = STRIP // DA_QK_DIM
    gq2 = jnp.tile(da_q_norm_g[0], reps).reshape(1, STRIP) * (DA_QK_DIM ** -0.5 * math.log2(math.e))
    gk2 = jnp.tile(da_k_norm_g[0], reps).reshape(1, STRIP)
    score_bound = DA_QK_DIM * jnp.max(jnp.abs(gq2)) * jnp.max(jnp.abs(gk2))
    bounded = (score_bound <= SCORE_BOUND).astype(jnp.int32).reshape(1)
    lam4 = jnp.stack([da_lambda_q1[0], da_lambda_k1[0], da_lambda_q2[0], da_lambda_k2[0]])
    bias8 = jnp.concatenate([ml_b_i[0], ml_b_f[0]])
    bias_row = jnp.pad(bias8, (0, LANES - N_GATES)).reshape(1, LANES)
    bias_col = bias8.reshape(N_GATES, 1)

    qk, vt, ml, gates, gates_t = _inproj(x2, norm1_g, w_main, w_gate, B, S,
                                         t["bm_in"], t["bn_in"], t["tp"])
    qt, kn = _prep(qk.reshape(B, S, 2 * DA_WIDTH), gq2, gk2, t["tp"])
    proj3 = ml.reshape(B, S, ml.shape[-1])
    gates3 = gates.reshape(B, S, LANES)

    d_out, m_out = _attn_mlstm(bounded, lam4, qt, kn, vt, da_out_g[0].reshape(DA_HEAD_DIM, 1),
                               proj3, gates3, gates_t, ml_conv_w[0], ml_conv_b, bias_row, bias_col,
                               ml_out_g[0], t["tq"], t["tp"])

    x1 = _outproj(x2, d_out.reshape(T, DA_WIDTH), m_out.reshape(T, ML_WIDTH),
                  w_out[0].astype(BF16), t["bm_out"])
    y = _mlp(x1, norm2_g, w_up[0].astype(BF16), w_down[0].astype(BF16), t["bm_mlp"], t["tf"])
    return y.reshape(B, S, D)
```
